```python
import math
import jax
import jax.numpy as jnp
from jax import lax
import numpy as np

D_MODEL = 1024
BATCH = 8
SEQ = 4096
DEPTH = 1
DEC_BATCH = 2
DEC_SEQ = 16384
PAST_LEN = 128

ATTN_WIDTH = D_MODEL // 2
HYENA_WIDTH = D_MODEL - ATTN_WIDTH
HEAD_DIM = 64
N_HEADS = ATTN_WIDTH // HEAD_DIM
DILATED_PATTERNS = ((128, 1), (512, 4), (2048, 16))
HYENA_ORDER = 2
SHORT_CONV_WIDTH = 3
FILTER_EMB_DIM = 33
FILTER_HIDDEN = 64
FILTER_FAST_DECAY_PCT = 0.3
FILTER_SLOW_DECAY_PCT = 1.5
FILTER_DECAY_TARGET = 1e-2
PROJ_WIDTH = 4 * ATTN_WIDTH + (HYENA_ORDER + 2) * HYENA_WIDTH
LN_EPS = 1e-5
RMS_EPS = 1e-6
NEG_INF = -1e30

kernel_name = 'hymba_hyena_dilated_alibi_deepnorm_encoder'


def _alibi_slopes(n):
    return jnp.asarray([2.0 ** (-8.0 * (i + 1) / n) for i in range(n)], jnp.float32)


def _dilated_window_attention(q, k, v, slopes, window, dilation):
    B, S, H, Dh = q.shape
    half = window // (2 * dilation)
    blk = half
    s_sub = S // dilation
    n_blk = -(-s_sub // blk)
    pad = n_blk * blk - s_sub
    N = B * dilation

    def to_sub(t):
        return t.reshape(B, s_sub, dilation, H, Dh).transpose(0, 2, 1, 3, 4).reshape(N, s_sub, H, Dh)

    qs, ks, vs = to_sub(q), to_sub(k), to_sub(v)
    qb = jnp.pad(qs, ((0, 0), (0, pad), (0, 0), (0, 0))).reshape(N, n_blk, blk, H, Dh)

    def neighbours(t):
        t = jnp.pad(t, ((0, 0), (blk, pad + blk), (0, 0), (0, 0))).reshape(N, n_blk + 2, blk, H, Dh)
        return jnp.concatenate([t[:, :-2], t[:, 1:-1], t[:, 2:]], axis=2)

    kb, vb = neighbours(ks), neighbours(vs)
    scores = jnp.einsum('nbqhd,nbkhd->nbhqk', qb, kb) * (Dh ** -0.5)
    q_idx = jnp.arange(n_blk)[:, None] * blk + jnp.arange(blk)[None, :]
    k_idx = jnp.arange(n_blk)[:, None] * blk - blk + jnp.arange(3 * blk)[None, :]
    rel = jnp.abs(k_idx[:, None, :] - q_idx[:, :, None])
    valid = (rel <= half) & (k_idx[:, None, :] >= 0) & (k_idx[:, None, :] < s_sub)
    dist = (rel * dilation).astype(jnp.float32)
    bias = -slopes[None, :, None, None] * dist[:, None]
    scores = jnp.where(valid[:, None], scores + bias, NEG_INF)
    m = jnp.max(scores, axis=-1, keepdims=True)
    p = jnp.exp(scores - m)
    l = jnp.sum(p, axis=-1, keepdims=True)
    o = jnp.einsum('nbhqk,nbkhd->nbqhd', p, vb) / jnp.swapaxes(l, 2, 3)
    lse = jnp.swapaxes((m + jnp.log(l))[..., 0], 2, 3)
    o = o.reshape(N, n_blk * blk, H, Dh)[:, :s_sub]
    o = o.reshape(B, dilation, s_sub, H, Dh).transpose(0, 2, 1, 3, 4).reshape(B, S, H, Dh)
    lse = lse.reshape(N, n_blk * blk, H)[:, :s_sub]
    lse = lse.reshape(B, dilation, s_sub, H).transpose(0, 2, 1, 3).reshape(B, S, H)
    return o, lse


def _hyena_filters(L, w1, b1, w2, b2, w3, b3, freq, w4):
    f32 = jnp.float32
    t = jnp.linspace(0.0, 1.0, L, dtype=f32)[:, None]
    bands = (FILTER_EMB_DIM - 1) // 2
    w = 2.0 * math.pi * jnp.arange(L, dtype=f32)[:, None] / L
    fr = jnp.linspace(1e-4, bands - 1, bands, dtype=f32)[None, :]
    z = jnp.concatenate([t, jnp.cos(fr * w), -jnp.sin(fr * w)], axis=-1)
    freq = freq.astype(f32)
    h = jnp.sin(freq[0] * (z @ w1.astype(f32) + b1.astype(f32)))
    h = jnp.sin(freq[1] * (h @ w2.astype(f32) + b2.astype(f32)))
    h = jnp.sin(freq[2] * (h @ w3.astype(f32) + b3.astype(f32)))
    h = (h @ w4.astype(f32)).reshape(L, HYENA_ORDER, 2, HYENA_WIDTH)
    max_decay = math.log(FILTER_DECAY_TARGET) / FILTER_FAST_DECAY_PCT
    min_decay = math.log(FILTER_DECAY_TARGET) / FILTER_SLOW_DECAY_PCT
    deltas = jnp.abs(jnp.linspace(min_decay, max_decay, HYENA_WIDTH, dtype=f32))
    decay = jnp.exp(-t * deltas[None, :])
    return h * decay[:, None, None, :]


def _two_sided_fftconv(u, h_fwd, h_bwd):
    L, C = h_fwd.shape
    filt = jnp.concatenate([h_fwd, jnp.zeros((1, C), h_fwd.dtype), h_bwd[1:][::-1]], axis=0)
    F = jnp.fft.rfft(filt, n=2 * L, axis=0)
    U = jnp.fft.rfft(u, n=2 * L, axis=1)
    return jnp.fft.irfft(U * F[None], n=2 * L, axis=1)[:, :L]


def _short_conv(u, w, b):
    r = SHORT_CONV_WIDTH // 2
    L = u.shape[1]
    up = jnp.pad(u, ((0, 0), (r, r), (0, 0)))
    out = b
    for j in range(SHORT_CONV_WIDTH):
        out = out + up[:, j:j + L] * w[j]
    return out


def _rms(y, g):
    y = y.astype(jnp.float32)
    return y * lax.rsqrt(jnp.mean(y * y, axis=-1, keepdims=True) + RMS_EPS) * g.astype(jnp.float32)


def _layer(x, w_in, conv_w, conv_b, filt_w1, filt_b1, filt_w2, filt_b2, filt_w3, filt_b3,
           filt_freq, filt_w4, hyena_d, attn_norm_g, hyena_norm_g, w_out, ln_g, ln_b):
    f32 = jnp.float32
    B, L, _ = x.shape
    A, C = ATTN_WIDTH, HYENA_WIDTH
    proj = x @ w_in
    q, k, v, g_attn = (proj[..., i * A:(i + 1) * A] for i in range(4))
    u_h = proj[..., 4 * A:4 * A + (HYENA_ORDER + 1) * C]
    g_hyena = proj[..., 4 * A + (HYENA_ORDER + 1) * C:]

    heads = lambda t: t.reshape(B, L, N_HEADS, HEAD_DIM).astype(f32)
    qh, kh, vh = heads(q), heads(k), heads(v)
    slopes = _alibi_slopes(N_HEADS)
    outs, lses = [], []
    for window, dilation in DILATED_PATTERNS:
        o, lse = _dilated_window_attention(qh, kh, vh, slopes, window, dilation)
        outs.append(o)
        lses.append(lse)
    wts = jax.nn.softmax(jnp.stack(lses), axis=0)
    attn = jnp.einsum('pblh,pblhd->blhd', wts, jnp.stack(outs)).reshape(B, L, A)

    u = _short_conv(u_h, conv_w, conv_b).astype(f32)
    z = u[..., :C]
    filters = _hyena_filters(L, filt_w1, filt_b1, filt_w2, filt_b2, filt_w3, filt_b3, filt_freq, filt_w4)
    for n in range(HYENA_ORDER):
        gate = u[..., (n + 1) * C:(n + 2) * C]
        z = gate * (_two_sided_fftconv(z, filters[:, n, 0], filters[:, n, 1]) + hyena_d[n].astype(f32) * z)

    mixed = jnp.concatenate([
        _rms(attn, attn_norm_g) * jax.nn.silu(g_attn.astype(f32)),
        _rms(z, hyena_norm_g) * jax.nn.silu(g_hyena.astype(f32)),
    ], axis=-1).astype(x.dtype)
    out = mixed @ w_out
    alpha = (2.0 * DEPTH) ** 0.25
    h = alpha * x.astype(f32) + out.astype(f32)
    mu = jnp.mean(h, axis=-1, keepdims=True)
    var = jnp.mean(jnp.square(h - mu), axis=-1, keepdims=True)
    y = (h - mu) * lax.rsqrt(var + LN_EPS) * ln_g.astype(f32) + ln_b.astype(f32)
    return y.astype(x.dtype)


def setup_inputs(seed: int = 0) -> dict:
    key = jax.random.key(seed)
    ks = jax.random.split(key, 24)
    f32 = jnp.float32
    nrm = lambda kk, shape, scale: jax.random.normal(kk, shape, f32) * scale
    beta = (8.0 * DEPTH) ** -0.25
    A, C = ATTN_WIDTH, HYENA_WIDTH
    col_scale = jnp.concatenate([
        jnp.ones((2 * A,), f32), jnp.full((A,), beta, f32), jnp.ones((A,), f32),
        jnp.full((C,), beta, f32), jnp.ones(((HYENA_ORDER + 1) * C,), f32)])
    return {
        'x_prompt': nrm(ks[0], (BATCH, SEQ, D_MODEL), 1.0),
        'x_sample': nrm(ks[1], (DEC_BATCH, DEC_SEQ, D_MODEL), 1.0),
        'w_in': nrm(ks[2], (DEPTH, D_MODEL, PROJ_WIDTH), D_MODEL ** -0.5) * col_scale,
        'conv_w': nrm(ks[3], (DEPTH, SHORT_CONV_WIDTH, (HYENA_ORDER + 1) * C), SHORT_CONV_WIDTH ** -0.5),
        'conv_b': nrm(ks[4], (DEPTH, (HYENA_ORDER + 1) * C), 0.01),
        'filt_w1': nrm(ks[5], (DEPTH, FILTER_EMB_DIM, FILTER_HIDDEN), FILTER_EMB_DIM ** -0.5),
        'filt_b1': nrm(ks[6], (DEPTH, FILTER_HIDDEN), 0.1),
        'filt_w2': nrm(ks[7], (DEPTH, FILTER_HIDDEN, FILTER_HIDDEN), FILTER_HIDDEN ** -0.5),
        'filt_b2': nrm(ks[8], (DEPTH, FILTER_HIDDEN), 0.1),
        'filt_w3': nrm(ks[9], (DEPTH, FILTER_HIDDEN, FILTER_HIDDEN), FILTER_HIDDEN ** -0.5),
        'filt_b3': nrm(ks[10], (DEPTH, FILTER_HIDDEN), 0.1),
        'filt_freq': 1.0 + nrm(ks[11], (DEPTH, 3, FILTER_HIDDEN), 0.01),
        'filt_w4': nrm(ks[12], (DEPTH, FILTER_HIDDEN, HYENA_ORDER * 2 * C), FILTER_HIDDEN ** -0.5),
        'hyena_d': nrm(ks[13], (DEPTH, HYENA_ORDER, C), 1.0),
        'attn_norm_g': 1.0 + nrm(ks[14], (DEPTH, A), 0.01),
        'hyena_norm_g': 1.0 + nrm(ks[15], (DEPTH, C), 0.01),
        'w_out': nrm(ks[16], (DEPTH, D_MODEL, D_MODEL), D_MODEL ** -0.5 * beta),
        'ln_g': 1.0 + nrm(ks[17], (DEPTH, D_MODEL), 0.01),
        'ln_b': nrm(ks[18], (DEPTH, D_MODEL), 0.01),
    }


def reference(x_prompt, x_sample, w_in, conv_w, conv_b, filt_w1, filt_b1, filt_w2, filt_b2,
              filt_w3, filt_b3, filt_freq, filt_w4, hyena_d, attn_norm_g, hyena_norm_g,
              w_out, ln_g, ln_b):
    def trunk(x):
        for l in range(DEPTH):
            x = _layer(x, w_in[l], conv_w[l], conv_b[l], filt_w1[l], filt_b1[l], filt_w2[l],
                       filt_b2[l], filt_w3[l], filt_b3[l], filt_freq[l], filt_w4[l], hyena_d[l],
                       attn_norm_g[l], hyena_norm_g[l], w_out[l], ln_g[l], ln_b[l])
        return x
    y_prompt = trunk(x_prompt)
    y_sample = trunk(x_sample)
    return (y_prompt, y_sample)
```

```python
import functools
import math

import numpy as np
import jax
import jax.numpy as jnp
from jax import lax
from jax.experimental import pallas as pl
from jax.experimental.pallas import tpu as pltpu

F32 = jnp.float32
BF16 = jnp.bfloat16

D_MODEL = 1024
ATTN_WIDTH = 512
HYENA_WIDTH = 512
HEAD_DIM = 64
N_HEADS = 8
PATTERNS = ((128, 1), (512, 4), (2048, 16))
HALF_WINDOW = 64
HYENA_ORDER = 2
FILTER_BANDS = 16
FILTER_HIDDEN = 64
FAST_DECAY_PCT = 0.3
SLOW_DECAY_PCT = 1.5
DECAY_TARGET = 1e-2
PROJ_WIDTH = 4096
LN_EPS = 1e-5
RMS_EPS = 1e-6
NEG_INF = -1e30
DEPTH = 1

Q_BLOCK = 128
KEY_WINDOW = 256
VMEM_LIMIT = 56 * 1024 * 1024


def _cparams(*sem):
    return pltpu.CompilerParams(dimension_semantics=sem, vmem_limit_bytes=VMEM_LIMIT)


def _silu(g):
    return g / (1.0 + jnp.exp(-g))


def _inproj_kernel(xp_ref, x_ref, xn_ref, w_ref, cw_ref, cb_ref,
                   qkvg_ref, z0_ref, x1_ref, x2_ref, gh_ref, *, tm, n_tiles):
    i = pl.program_id(1)
    a, c = ATTN_WIDTH, HYENA_WIDTH
    xb = x_ref[0].astype(BF16)

    def proj(lo, hi, lhs=xb):
        return jnp.dot(lhs, w_ref[:, lo:hi], preferred_element_type=F32)

    qkvg_ref[0, :, 0:a] = (proj(0, a) * (HEAD_DIM ** -0.5)).astype(BF16)
    qkvg_ref[0, :, a:3 * a] = proj(a, 3 * a).astype(BF16)
    qkvg_ref[0, :, 3 * a:4 * a] = _silu(proj(3 * a, 4 * a)).astype(BF16)
    gh_ref[0] = _silu(proj(4 * a + 3 * c, 4 * a + 4 * c)).astype(BF16)

    u0, u1 = 4 * a, 4 * a + 3 * c
    p = proj(u0, u1)
    prev_row = proj(u0, u1, xp_ref[0, 0].astype(BF16))[7:8, :]
    next_row = proj(u0, u1, xn_ref[0, 0].astype(BF16))[0:1, :]
    prev_row = prev_row * (i > 0).astype(F32)
    next_row = next_row * (i < n_tiles - 1).astype(F32)
    row = lax.broadcasted_iota(jnp.int32, p.shape, 0)
    below = jnp.where(row == 0, prev_row, pltpu.roll(p, 1, 0))
    above = jnp.where(row == tm - 1, next_row, pltpu.roll(p, tm - 1, 0))
    u = cb_ref[...] + below * cw_ref[0:1, :] + p * cw_ref[1:2, :] + above * cw_ref[2:3, :]
    z0_ref[0] = u[:, 0:c].astype(BF16)
    x1_ref[0] = u[:, c:2 * c].astype(BF16)
    x2_ref[0] = u[:, 2 * c:3 * c].astype(BF16)


def _inproj(x, w_bf, conv_w, conv_b, *, tm):
    b, l, d = x.shape
    n_tiles = l // tm
    x8 = x.reshape(b, l // 8, 8, d)
    r = tm // 8
    n8 = l // 8
    bf_out = lambda w: jax.ShapeDtypeStruct((b, l, w), BF16)
    row_spec = lambda w: pl.BlockSpec((1, tm, w), lambda bi, i: (bi, i, 0))
    return pl.pallas_call(
        functools.partial(_inproj_kernel, tm=tm, n_tiles=n_tiles),
        grid=(b, n_tiles),
        in_specs=[
            pl.BlockSpec((1, 1, 8, d), lambda bi, i: (bi, jnp.maximum(i * r - 1, 0), 0, 0)),
            pl.BlockSpec((1, tm, d), lambda bi, i: (bi, i, 0)),
            pl.BlockSpec((1, 1, 8, d), lambda bi, i: (bi, jnp.minimum((i + 1) * r, n8 - 1), 0, 0)),
            pl.BlockSpec((d, PROJ_WIDTH), lambda bi, i: (0, 0)),
            pl.BlockSpec((3, 3 * HYENA_WIDTH), lambda bi, i: (0, 0)),
            pl.BlockSpec((1, 3 * HYENA_WIDTH), lambda bi, i: (0, 0)),
        ],
        out_specs=[row_spec(4 * ATTN_WIDTH), row_spec(HYENA_WIDTH), row_spec(HYENA_WIDTH),
                   row_spec(HYENA_WIDTH), row_spec(HYENA_WIDTH)],
        out_shape=[bf_out(4 * ATTN_WIDTH), bf_out(HYENA_WIDTH), bf_out(HYENA_WIDTH),
                   bf_out(HYENA_WIDTH), bf_out(HYENA_WIDTH)],
        compiler_params=_cparams("parallel", "arbitrary"),
        name="inproj",
    )(x8, x, x8, w_bf, conv_w, conv_b)


def _attn_kernel(q_ref, kp_ref, k_ref, kn_ref, vp_ref, v_ref, vn_ref, bias_ref,
                 o_ref, lse_ref, *, tq, s_sub):
    i = pl.program_id(2)
    kext = jnp.concatenate([kp_ref[0], k_ref[0], kn_ref[0]], axis=0)
    vext = jnp.concatenate([vp_ref[0], v_ref[0], vn_ref[0]], axis=0)
    for js in range(0, tq, Q_BLOCK):
        kidx = i * tq + js - HALF_WINDOW + lax.broadcasted_iota(jnp.int32, (1, KEY_WINDOW), 1)
        kvalid = (kidx >= 0) & (kidx < s_sub)
        kw = kext[js:js + KEY_WINDOW]
        vw = vext[js:js + KEY_WINDOW]
        for h in range(N_HEADS):
            hs = slice(h * HEAD_DIM, (h + 1) * HEAD_DIM)
            qh = q_ref[0, js:js + Q_BLOCK, hs]
            s = lax.dot_general(qh, kw[:, hs], (((1,), (1,)), ((), ())),
                                preferred_element_type=F32)
            s = jnp.where(kvalid, s + bias_ref[h], NEG_INF)
            m = jnp.max(s, axis=-1, keepdims=True)
            p = jnp.exp(s - m)
            l = jnp.sum(p, axis=-1, keepdims=True)
            o = jnp.dot(p.astype(BF16), vw[:, hs], preferred_element_type=F32) / l
            o_ref[0, js:js + Q_BLOCK, hs] = o.astype(BF16)
            lse_ref[0, js:js + Q_BLOCK, hs] = jnp.broadcast_to(m + jnp.log(l), (Q_BLOCK, HEAD_DIM))


def _band_bias(dilation):
    qi = np.arange(Q_BLOCK)[:, None]
    kj = np.arange(KEY_WINDOW)[None, :]
    rel = np.abs(kj - HALF_WINDOW - qi)
    slopes = np.asarray([2.0 ** (-8.0 * (i + 1) / N_HEADS) for i in range(N_HEADS)], np.float32)
    bias = -slopes[:, None, None] * (rel * dilation).astype(np.float32)[None]
    return jnp.asarray(np.where(rel[None] <= HALF_WINDOW, bias, NEG_INF).astype(np.float32))


def _attention(qkvg, dilation):
    b, l, w4 = qkvg.shape
    a = ATTN_WIDTH
    s_sub = l // dilation
    tq = min(256, s_sub)
    hw = HALF_WINDOW
    view = qkvg.reshape(b, s_sub, dilation * w4)
    cpb = w4 // a
    n_halo = s_sub // hw
    rq = tq // hw

    def main(col):
        return pl.BlockSpec((1, tq, a), lambda bi, r, i: (bi, i, r * cpb + col))

    def prev(col):
        return pl.BlockSpec((1, hw, a), lambda bi, r, i: (bi, jnp.maximum(i * rq - 1, 0), r * cpb + col))

    def nxt(col):
        return pl.BlockSpec((1, hw, a),
                            lambda bi, r, i: (bi, jnp.minimum((i + 1) * rq, n_halo - 1), r * cpb + col))

    out_spec = pl.BlockSpec((1, tq, a), lambda bi, r, i: (bi, i, r))
    o, lse = pl.pallas_call(
        functools.partial(_attn_kernel, tq=tq, s_sub=s_sub),
        grid=(b, dilation, s_sub // tq),
        in_specs=[main(0), prev(1), main(1), nxt(1), prev(2), main(2), nxt(2),
                  pl.BlockSpec((N_HEADS, Q_BLOCK, KEY_WINDOW), lambda bi, r, i: (0, 0, 0))],
        out_specs=[out_spec, out_spec],
        out_shape=[jax.ShapeDtypeStruct((b, s_sub, dilation * a), BF16),
                   jax.ShapeDtypeStruct((b, s_sub, dilation * a), F32)],
        compiler_params=_cparams("parallel", "parallel", "arbitrary"),
        name=f"attn_d{dilation}",
    )(view, view, view, view, view, view, view, _band_bias(dilation))
    return o.reshape(b, l, a), lse.reshape(b, l, a)


def _filter_kernel(w1_ref, b1_ref, w2_ref, b2_ref, w3_ref, b3_ref, fq_ref, w4_ref, fr_ref, dl_ref,
                   out_ref, *, tn, l):
    hp = lax.Precision.HIGHEST
    c = HYENA_WIDTH
    n = pl.program_id(0) * tn + lax.broadcasted_iota(jnp.int32, (tn, 1), 0)
    pos = jnp.where(n < l, n, 2 * l - n).astype(F32)
    t = pos * (1.0 / (l - 1))
    ang = (2.0 * math.pi / l) * pos * fr_ref[...]
    nb = FILTER_BANDS
    h = (t * w1_ref[0:1, :]
         + jnp.dot(jnp.cos(ang), w1_ref[1:1 + nb, :], precision=hp, preferred_element_type=F32)
         - jnp.dot(jnp.sin(ang), w1_ref[1 + nb:1 + 2 * nb, :], precision=hp, preferred_element_type=F32))
    h = jnp.sin(fq_ref[0:1, :] * (h + b1_ref[...]))
    h = jnp.sin(fq_ref[1:2, :] * (jnp.dot(h, w2_ref[...], precision=hp, preferred_element_type=F32) + b2_ref[...]))
    h = jnp.sin(fq_ref[2:3, :] * (jnp.dot(h, w3_ref[...], precision=hp, preferred_element_type=F32) + b3_ref[...]))
    decay = jnp.exp(-t * dl_ref[...])
    fwd = n < l
    scale = jnp.where(fwd, decay, jnp.where(n == l, 0.0, -decay))
    for o in range(HYENA_ORDER):
        base = o * 2 * c
        hf = jnp.dot(h, w4_ref[:, base:base + c], precision=hp, preferred_element_type=F32)
        hb = jnp.dot(h, w4_ref[:, base + c:base + 2 * c], precision=hp, preferred_element_type=F32)
        out_ref[o] = (jnp.where(fwd, hf, hb) * scale).astype(BF16)


def _filters(l, w1, b1, w2, b2, w3, b3, freq, w4):
    n = 2 * l
    tn = min(512, n)
    bands = FILTER_BANDS
    fr = jnp.asarray(np.linspace(1e-4, bands - 1, bands, dtype=np.float32)[None, :])
    max_decay = math.log(DECAY_TARGET) / FAST_DECAY_PCT
    min_decay = math.log(DECAY_TARGET) / SLOW_DECAY_PCT
    deltas = jnp.asarray(np.abs(np.linspace(min_decay, max_decay, HYENA_WIDTH, dtype=np.float32))[None, :])
    full = lambda arr: pl.BlockSpec(arr.shape, lambda i: (0,) * arr.ndim)
    args = (w1, b1[None, :], w2, b2[None, :], w3, b3[None, :], freq, w4, fr, deltas)
    return pl.pallas_call(
        functools.partial(_filter_kernel, tn=tn, l=l),
        grid=(n // tn,),
        in_specs=[full(a) for a in args],
        out_specs=pl.BlockSpec((HYENA_ORDER, tn, HYENA_WIDTH), lambda i: (0, i, 0)),
        out_shape=jax.ShapeDtypeStruct((HYENA_ORDER, n, HYENA_WIDTH), BF16),
        compiler_params=_cparams("parallel"),
        name=f"filters_l{l}",
    )(*args)


def _fft_plan(l):
    n = 2 * l
    na = 256 if n >= 32768 else (128 if n >= 8192 else 64)
    nb = n // na
    return na, nb, min(8, nb), min(8, na // 2)


@functools.lru_cache(maxsize=None)
def _dft_tables(l):
    na, nb, _, _ = _fft_plan(l)
    n = na * nb
    ka = np.arange(na // 2, dtype=np.int64)
    nas = np.arange(na, dtype=np.int64)
    nbs = np.arange(nb, dtype=np.int64)
    num = ((2 * ka[None, :, None] + 1) * (nb * nas[None, None, :] + nbs[:, None, None])) % (2 * n)
    theta = num.astype(np.float64) * (math.pi / n)
    fwd = np.concatenate([np.cos(theta), -np.sin(theta)], axis=1)
    thetat = np.swapaxes(theta[:, :, :na // 2], 1, 2)
    inv = (2.0 / n) * np.concatenate([np.cos(thetat), -np.sin(thetat)], axis=2)
    phi = ((nbs[:, None] * nbs[None, :]) % nb).astype(np.float64) * (2.0 * math.pi / nb)
    cc, ss = np.cos(phi), np.sin(phi)
    e2 = np.block([[cc, ss], [-ss, cc]])
    e2i = np.block([[cc, -ss], [ss, cc]])
    cast = lambda arr: np.ascontiguousarray(arr.astype(np.float32))
    return cast(fwd[:, :, :na // 2]), cast(fwd), cast(inv), cast(e2), cast(e2i)


def _s1_kernel(src_ref, m_ref, out_ref, *, tb):
    c = HYENA_WIDTH
    for j in range(tb):
        cs = slice(j * c, (j + 1) * c)
        out_ref[0, :, cs] = jnp.dot(m_ref[j], src_ref[0, :, cs], preferred_element_type=F32).astype(BF16)


def _stage1(src, mats, *, nb, tb):
    bx, rows, c = src.shape
    n_in = rows // nb
    na = mats.shape[1]
    view = src.reshape(bx, n_in, nb * c)
    return pl.pallas_call(
        functools.partial(_s1_kernel, tb=tb),
        grid=(nb // tb, bx),
        in_specs=[pl.BlockSpec((1, n_in, tb * c), lambda j, bi: (bi, 0, j)),
                  pl.BlockSpec((tb, na, n_in), lambda j, bi: (j, 0, 0))],
        out_specs=pl.BlockSpec((1, na, tb * c), lambda j, bi: (bi, 0, j)),
        out_shape=jax.ShapeDtypeStruct((bx, na, nb * c), BF16),
        compiler_params=_cparams("parallel", "arbitrary"),
        name=f"dft1_r{rows}",
    )(view, mats)


def _s2f_kernel(y_ref, e2_ref, h_ref, *, tk):
    for t in range(tk):
        d = jnp.concatenate([y_ref[0, 0, t], y_ref[0, 1, t]], axis=0)
        h_ref[0, t] = jnp.dot(e2_ref[...], d, preferred_element_type=F32)


def _filter_spectrum(y1, e2, *, na, nb, tk):
    bx = y1.shape[0]
    c = HYENA_WIDTH
    y5 = y1.reshape(bx, 2, na // 2, nb, c)
    return pl.pallas_call(
        functools.partial(_s2f_kernel, tk=tk),
        grid=(bx, na // 2 // tk),
        in_specs=[pl.BlockSpec((1, 2, tk, nb, c), lambda o, k: (o, 0, k, 0, 0)),
                  pl.BlockSpec((2 * nb, 2 * nb), lambda o, k: (0, 0))],
        out_specs=pl.BlockSpec((1, tk, 2 * nb, c), lambda o, k: (o, k, 0, 0)),
        out_shape=jax.ShapeDtypeStruct((bx, na // 2, 2 * nb, c), F32),
        compiler_params=_cparams("parallel", "arbitrary"),
        name=f"filter_spec_n{na * nb}",
    )(y5, e2)


def _s2_kernel(y_ref, h_ref, e2_ref, e2i_ref, out_ref, *, tk, nb):
    for t in range(tk):
        d = jnp.concatenate([y_ref[0, 0, t], y_ref[0, 1, t]], axis=0)
        y = jnp.dot(e2_ref[...], d, preferred_element_type=F32)
        yr, yi = y[:nb], y[nb:]
        hr, hi = h_ref[0, t, :nb], h_ref[0, t, nb:]
        p = jnp.concatenate([yr * hr - yi * hi, yr * hi + yi * hr], axis=0).astype(BF16)
        o = jnp.dot(e2i_ref[...], p, preferred_element_type=F32)
        out_ref[0, 0, t] = o[:nb].astype(BF16)
        out_ref[0, 1, t] = o[nb:].astype(BF16)


def _stage2(y1, hspec, order, e2, e2i, *, na, nb, tk):
    bx = y1.shape[0]
    c = HYENA_WIDTH
    y5 = y1.reshape(bx, 2, na // 2, nb, c)
    yspec = pl.BlockSpec((1, 2, tk, nb, c), lambda k, bi: (bi, 0, k, 0, 0))
    mat = pl.BlockSpec((2 * nb, 2 * nb), lambda k, bi: (0, 0))
    out = pl.pallas_call(
        functools.partial(_s2_kernel, tk=tk, nb=nb),
        grid=(na // 2 // tk, bx),
        in_specs=[yspec, pl.BlockSpec((1, tk, 2 * nb, c), lambda k, bi: (order, k, 0, 0)), mat, mat],
        out_specs=yspec,
        out_shape=jax.ShapeDtypeStruct((bx, 2, na // 2, nb, c), BF16),
        compiler_params=_cparams("parallel", "arbitrary"),
        name=f"dft2_n{na * nb}",
    )(y5, hspec, e2, e2i)
    return out.reshape(bx, na, nb * c)


def _s1inv_kernel(y_ref, g_ref, z_ref, gate_ref, d_ref, *rest, tb, chain):
    c = HYENA_WIDTH
    if chain:
        m_ref, znew_ref, y1_ref = rest
    else:
        (znew_ref,) = rest
    for j in range(tb):
        cs = slice(j * c, (j + 1) * c)
        conv = jnp.dot(g_ref[j], y_ref[0, :, cs], preferred_element_type=F32)
        zn = gate_ref[0, :, cs].astype(F32) * (conv + d_ref[...] * z_ref[0, :, cs].astype(F32))
        znb = zn.astype(BF16)
        znew_ref[0, :, cs] = znb
        if chain:
            y1_ref[0, :, cs] = jnp.dot(m_ref[j], znb, preferred_element_type=F32).astype(BF16)


def _stage1_inverse(y3, ginv, z, gate, d, mats, *, na, nb, tb):
    bx, l, c = z.shape
    half = na // 2
    chain = mats is not None
    blk = lambda rows: pl.BlockSpec((1, rows, tb * c), lambda j, bi: (bi, 0, j))
    in_specs = [blk(na), pl.BlockSpec((tb, half, na), lambda j, bi: (j, 0, 0)), blk(half), blk(half),
                pl.BlockSpec((1, c), lambda j, bi: (0, 0))]
    args = [y3, ginv, z.reshape(bx, half, nb * c), gate.reshape(bx, half, nb * c), d]
    out_specs = [blk(half)]
    out_shape = [jax.ShapeDtypeStruct((bx, half, nb * c), BF16)]
    if chain:
        in_specs.append(pl.BlockSpec((tb, na, half), lambda j, bi: (j, 0, 0)))
        args.append(mats)
        out_specs.append(blk(na))
        out_shape.append(jax.ShapeDtypeStruct((bx, na, nb * c), BF16))
    outs = pl.pallas_call(
        functools.partial(_s1inv_kernel, tb=tb, chain=chain),
        grid=(nb // tb, bx),
        in_specs=in_specs, out_specs=out_specs, out_shape=out_shape,
        compiler_params=_cparams("parallel", "arbitrary"),
        name=f"idft1_l{l}" + ("_chain" if chain else ""),
    )(*args)
    znew = outs[0].reshape(bx, l, c)
    return (znew, outs[1]) if chain else (znew, None)


def _hyena(z0, x1, x2, filt, hyena_d):
    bx, l, c = z0.shape
    na, nb, tb, tk = _fft_plan(l)
    m_data, m_full, ginv, e2, e2i = (jnp.asarray(t).astype(BF16) for t in _dft_tables(l))
    hspec = _filter_spectrum(_stage1(filt, m_full, nb=nb, tb=tb), e2, na=na, nb=nb, tk=tk)
    y1 = _stage1(z0, m_data, nb=nb, tb=tb)
    y3 = _stage2(y1, hspec, 0, e2, e2i, na=na, nb=nb, tk=tk)
    z1, y1 = _stage1_inverse(y3, ginv, z0, x1, hyena_d[0:1], m_data, na=na, nb=nb, tb=tb)
    y3 = _stage2(y1, hspec, 1, e2, e2i, na=na, nb=nb, tk=tk)
    z2, _ = _stage1_inverse(y3, ginv, z1, x2, hyena_d[1:2], None, na=na, nb=nb, tb=tb)
    return z2


def _final_kernel(o1_ref, o2_ref, o3_ref, l1_ref, l2_ref, l3_ref, ga_ref, z_ref, gh_ref, x_ref,
                  ag_ref, hg_ref, wo_ref, lg_ref, lb_ref, y_ref):
    l1, l2, l3 = l1_ref[0], l2_ref[0], l3_ref[0]
    mx = jnp.maximum(jnp.maximum(l1, l2), l3)
    e1, e2, e3 = jnp.exp(l1 - mx), jnp.exp(l2 - mx), jnp.exp(l3 - mx)
    attn = (e1 * o1_ref[0].astype(F32) + e2 * o2_ref[0].astype(F32) + e3 * o3_ref[0].astype(F32)) / (e1 + e2 + e3)

    def rms(v, g):
        return v * lax.rsqrt(jnp.mean(v * v, axis=-1, keepdims=True) + RMS_EPS) * g

    ma = (rms(attn, ag_ref[...]) * ga_ref[0].astype(F32)).astype(BF16)
    mh = (rms(z_ref[0].astype(F32), hg_ref[...]) * gh_ref[0].astype(F32)).astype(BF16)
    a = ATTN_WIDTH
    out = (jnp.dot(ma, wo_ref[0:a, :], preferred_element_type=F32)
           + jnp.dot(mh, wo_ref[a:, :], preferred_element_type=F32))
    h = ((2.0 * DEPTH) ** 0.25) * x_ref[0] + out
    mu = jnp.mean(h, axis=-1, keepdims=True)
    hc = h - mu
    var = jnp.mean(hc * hc, axis=-1, keepdims=True)
    y_ref[0] = hc * lax.rsqrt(var + LN_EPS) * lg_ref[...] + lb_ref[...]


def _final(os_, lses, qkvg, z, gh, x, attn_g, hyena_g, w_out_bf, ln_g, ln_b, *, tt):
    b, l, d = x.shape
    a = ATTN_WIDTH
    tok = lambda w, col=0: pl.BlockSpec((1, tt, w), lambda bi, i: (bi, i, col))
    vec = lambda w: pl.BlockSpec((1, w), lambda bi, i: (0, 0))
    return pl.pallas_call(
        _final_kernel,
        grid=(b, l // tt),
        in_specs=[tok(a), tok(a), tok(a), tok(a), tok(a), tok(a), tok(a, 3), tok(a), tok(a), tok(d),
                  vec(a), vec(a), pl.BlockSpec((d, d), lambda bi, i: (0, 0)), vec(d), vec(d)],
        out_specs=tok(d),
        out_shape=jax.ShapeDtypeStruct((b, l, d), F32),
        compiler_params=_cparams("parallel", "arbitrary"),
        name="merge_outproj_ln",
    )(*os_, *lses, qkvg, z, gh, x, attn_g[None, :], hyena_g[None, :], w_out_bf, ln_g[None, :], ln_b[None, :])


def _layer(x, w_in_bf, conv_w, conv_b, filt_w1, filt_b1, filt_w2, filt_b2, filt_w3, filt_b3,
           filt_freq, filt_w4, hyena_d, attn_norm_g, hyena_norm_g, w_out_bf, ln_g, ln_b):
    b, l, _ = x.shape
    tm = min(512, l)
    qkvg, z0, x1, x2, gh = _inproj(x, w_in_bf, conv_w, conv_b[None, :], tm=tm)
    outs = [_attention(qkvg, dil) for _, dil in PATTERNS]
    filt = _filters(l, filt_w1, filt_b1, filt_w2, filt_b2, filt_w3, filt_b3, filt_freq, filt_w4)
    z = _hyena(z0, x1, x2, filt, hyena_d)
    return _final([o for o, _ in outs], [s for _, s in outs], qkvg, z, gh, x,
                  attn_norm_g, hyena_norm_g, w_out_bf, ln_g, ln_b, tt=tm)


def kernel(x_prompt, x_sample, w_in, conv_w, conv_b, filt_w1, filt_b1, filt_w2, filt_b2, filt_w3,
           filt_b3, filt_freq, filt_w4, hyena_d, attn_norm_g, hyena_norm_g, w_out, ln_g, ln_b):
    def trunk(x):
        for i in range(DEPTH):
            x = _layer(x, w_in[i].astype(BF16), conv_w[i], conv_b[i], filt_w1[i], filt_b1[i], filt_w2[i],
                       filt_b2[i], filt_w3[i], filt_b3[i], filt_freq[i], filt_w4[i], hyena_d[i],
                       attn_norm_g[i], hyena_norm_g[i], w_out[i].astype(BF16), ln_g[i], ln_b[i])
        return x
    return (trunk(x_prompt), trunk(x_sample))
```

```python
import functools
import math

import numpy as np
import jax
import jax.numpy as jnp
from jax import lax
from jax.experimental import pallas as pl
from jax.experimental.pallas import tpu as pltpu

F32 = jnp.float32
BF16 = jnp.bfloat16

D_MODEL = 1024
ATTN_WIDTH = 512
HYENA_WIDTH = 512
HEAD_DIM = 64
N_HEADS = 8
HALF_WINDOW = 64
HYENA_ORDER = 2
FILTER_BANDS = 16
FILTER_HIDDEN = 64
FAST_DECAY_PCT = 0.3
SLOW_DECAY_PCT = 1.5
DECAY_TARGET = 1e-2
PROJ_WIDTH = 4096
LN_EPS = 1e-5
RMS_EPS = 1e-6
NEG_INF = -1e30
DEPTH = 1

Q_BLOCK = 128
KEY_WINDOW = 256
SLAB = 16
HEAD_GROUP = 4
GROUP_WIDTH = HEAD_GROUP * HEAD_DIM
VMEM_LIMIT = 56 * 1024 * 1024


def _cparams(*sem):
    return pltpu.CompilerParams(dimension_semantics=sem, vmem_limit_bytes=VMEM_LIMIT)


def _silu(g):
    return g / (1.0 + jnp.exp(-g))


def _inproj_kernel(xp_ref, x_ref, xn_ref, w_ref, cw_ref, cb_ref,
                   qkvg_ref, z0_ref, x1_ref, x2_ref, gh_ref, *, tm, n_tiles):
    i = pl.program_id(1)
    a, c = ATTN_WIDTH, HYENA_WIDTH
    xb = x_ref[0].astype(BF16)

    def proj(lo, hi, lhs=xb):
        return jnp.dot(lhs, w_ref[:, lo:hi], preferred_element_type=F32)

    qkvg_ref[0, :, 0:a] = (proj(0, a) * (HEAD_DIM ** -0.5)).astype(BF16)
    qkvg_ref[0, :, a:3 * a] = proj(a, 3 * a).astype(BF16)
    qkvg_ref[0, :, 3 * a:4 * a] = _silu(proj(3 * a, 4 * a)).astype(BF16)
    gh_ref[0] = _silu(proj(4 * a + 3 * c, 4 * a + 4 * c)).astype(BF16)

    u0, u1 = 4 * a, 4 * a + 3 * c
    p = proj(u0, u1)
    prev_row = proj(u0, u1, xp_ref[0, 0].astype(BF16))[7:8, :]
    next_row = proj(u0, u1, xn_ref[0, 0].astype(BF16))[0:1, :]
    prev_row = prev_row * (i > 0).astype(F32)
    next_row = next_row * (i < n_tiles - 1).astype(F32)
    row = lax.broadcasted_iota(jnp.int32, p.shape, 0)
    below = jnp.where(row == 0, prev_row, pltpu.roll(p, 1, 0))
    above = jnp.where(row == tm - 1, next_row, pltpu.roll(p, tm - 1, 0))
    u = cb_ref[...] + below * cw_ref[0:1, :] + p * cw_ref[1:2, :] + above * cw_ref[2:3, :]
    z0_ref[0] = u[:, 0:c].astype(BF16)
    x1_ref[0] = u[:, c:2 * c].astype(BF16)
    x2_ref[0] = u[:, 2 * c:3 * c].astype(BF16)


def _inproj(x, w_bf, conv_w, conv_b, *, tm):
    b, l, d = x.shape
    n_tiles = l // tm
    x8 = x.reshape(b, l // 8, 8, d)
    r = tm // 8
    n8 = l // 8
    bf_out = lambda w: jax.ShapeDtypeStruct((b, l, w), BF16)
    row_spec = lambda w: pl.BlockSpec((1, tm, w), lambda bi, i: (bi, i, 0))
    return pl.pallas_call(
        functools.partial(_inproj_kernel, tm=tm, n_tiles=n_tiles),
        grid=(b, n_tiles),
        in_specs=[
            pl.BlockSpec((1, 1, 8, d), lambda bi, i: (bi, jnp.maximum(i * r - 1, 0), 0, 0)),
            pl.BlockSpec((1, tm, d), lambda bi, i: (bi, i, 0)),
            pl.BlockSpec((1, 1, 8, d), lambda bi, i: (bi, jnp.minimum((i + 1) * r, n8 - 1), 0, 0)),
            pl.BlockSpec((d, PROJ_WIDTH), lambda bi, i: (0, 0)),
            pl.BlockSpec((3, 3 * HYENA_WIDTH), lambda bi, i: (0, 0)),
            pl.BlockSpec((1, 3 * HYENA_WIDTH), lambda bi, i: (0, 0)),
        ],
        out_specs=[row_spec(4 * ATTN_WIDTH), row_spec(HYENA_WIDTH), row_spec(HYENA_WIDTH),
                   row_spec(HYENA_WIDTH), row_spec(HYENA_WIDTH)],
        out_shape=[bf_out(4 * ATTN_WIDTH), bf_out(HYENA_WIDTH), bf_out(HYENA_WIDTH),
                   bf_out(HYENA_WIDTH), bf_out(HYENA_WIDTH)],
        compiler_params=_cparams("parallel", "arbitrary"),
        name="inproj",
    )(x8, x, x8, w_bf, conv_w, conv_b)


def _band_bias_tables():
    qi = np.arange(Q_BLOCK)[:, None]
    kj = np.arange(KEY_WINDOW)[None, :]
    rel_band = kj - HALF_WINDOW - qi
    rel_d4 = 4 * ((kj % 64) - 16 - (qi % 32)) + (kj // 64 - qi // 32)
    slopes = np.asarray([2.0 ** (-8.0 * (i + 1) / N_HEADS) for i in range(N_HEADS)], np.float32)
    out = []
    for rel, dil in ((rel_band, 1), (rel_d4, 4), (rel_band, 16)):
        rel = np.abs(rel)
        bias = -slopes[:, None, None] * (rel * dil).astype(np.float32)[None]
        out.append(np.where(rel[None] <= HALF_WINDOW, bias, NEG_INF).astype(np.float32))
    return np.stack(out)


def _attn_kernel(q_ref, kp_ref, k_ref, kn_ref, vp_ref, v_ref, vn_ref, bias_ref, o_ref,
                 qt_ref, kt_ref, vt_ref, kn3_ref, vn3_ref, acc_ref, m_ref, l_ref, on_ref, ln_ref,
                 *, ti, s16, seq_len):
    i0 = pl.program_id(1) * ti
    hw = HALF_WINDOW
    dn_t = (((1,), (1,)), ((), ()))

    qt_ref[...] = jnp.swapaxes(q_ref[0], 0, 1)
    kt_ref[...] = jnp.swapaxes(jnp.concatenate([kp_ref[0], k_ref[0], kn_ref[0]], axis=0), 0, 1)
    vt_ref[...] = jnp.swapaxes(jnp.concatenate([vp_ref[0], v_ref[0], vn_ref[0]], axis=0), 0, 1)
    kn3_ref[...] = jnp.concatenate([kp_ref[0, hw - 4:hw], k_ref[0], kn_ref[0, 0:4]], axis=0)
    vn3_ref[...] = jnp.concatenate([vp_ref[0, hw - 4:hw], v_ref[0], vn_ref[0, 0:4]], axis=0)

    col = lax.broadcasted_iota(jnp.int32, (1, KEY_WINDOW), 1)

    def tile(q, kw, vw, pat, kvalid):
        res = []
        for h in range(HEAD_GROUP):
            hs = slice(h * HEAD_DIM, (h + 1) * HEAD_DIM)
            s = lax.dot_general(q[:, hs], kw[:, hs], dn_t, preferred_element_type=F32)
            s = jnp.where(kvalid, s + bias_ref[pat, h], NEG_INF)
            m = jnp.max(s, axis=-1, keepdims=True)
            p = jnp.exp(s - m)
            l = jnp.sum(p, axis=-1, keepdims=True)
            acc = jnp.dot(p.astype(BF16), vw[:, hs], preferred_element_type=F32)
            res.append((m, l, acc))
        return res

    def widen(parts, k):
        return jnp.concatenate([jnp.broadcast_to(p[k], (Q_BLOCK, HEAD_DIM)) if k < 2 else p[k]
                                for p in parts], axis=1)

    i16 = i0 - hw + col

    def d16_body(t, carry):
        r = t // (ti // Q_BLOCK)
        js = pl.multiple_of((t % (ti // Q_BLOCK)) * Q_BLOCK, Q_BLOCK)
        parts = tile(qt_ref[r, pl.ds(js, Q_BLOCK), :], kt_ref[r, pl.ds(js, KEY_WINDOW), :],
                     vt_ref[r, pl.ds(js, KEY_WINDOW), :], 2, (i16 + js >= 0) & (i16 + js < s16))
        m_ref[r, pl.ds(js, Q_BLOCK), :] = widen(parts, 0)
        l_ref[r, pl.ds(js, Q_BLOCK), :] = widen(parts, 1)
        acc_ref[r, pl.ds(js, Q_BLOCK), :] = widen(parts, 2)
        return carry

    lax.fori_loop(0, SLAB * (ti // Q_BLOCK), d16_body, 0)

    n_sub = ti // 32

    def d4_body(t, carry):
        r4 = t // n_sub
        is_ = pl.multiple_of((t % n_sub) * 32, 32)
        ks = pl.multiple_of(is_ + hw - 16, 16)
        q = jnp.concatenate([qt_ref[r4 + 4 * q4, pl.ds(is_, 32), :] for q4 in range(4)], axis=0)
        kw = jnp.concatenate([kt_ref[r4 + 4 * q4, pl.ds(ks, 64), :] for q4 in range(4)], axis=0)
        vw = jnp.concatenate([vt_ref[r4 + 4 * q4, pl.ds(ks, 64), :] for q4 in range(4)], axis=0)
        i4 = i0 + is_ - 16 + (col % 64)
        parts = tile(q, kw, vw, 1, (i4 >= 0) & (i4 < s16))
        m4, l4, a4 = widen(parts, 0), widen(parts, 1), widen(parts, 2)
        for q4 in range(4):
            rows = slice(q4 * 32, (q4 + 1) * 32)
            at = (r4 + 4 * q4, pl.ds(is_, 32), slice(None))
            m_old, l_old, a_old = m_ref[at], l_ref[at], acc_ref[at]
            m_new = jnp.maximum(m_old, m4[rows])
            alpha, beta = jnp.exp(m_old - m_new), jnp.exp(m4[rows] - m_new)
            m_ref[at] = m_new
            l_ref[at] = l_old * alpha + l4[rows] * beta
            acc_ref[at] = a_old * alpha + a4[rows] * beta
        return carry

    lax.fori_loop(0, 4 * n_sub, d4_body, 0)

    on_ref[...] = jnp.swapaxes((acc_ref[...] / l_ref[...]).astype(BF16), 0, 1)
    ln_ref[...] = jnp.swapaxes(m_ref[...] + jnp.log(l_ref[...]), 0, 1)

    def d1_body(j, carry):
        s8 = pl.multiple_of(j * 8, 8)
        q = q_ref[0, pl.ds(s8, 8)].reshape(Q_BLOCK, GROUP_WIDTH)
        kw = kn3_ref[pl.ds(s8, SLAB)].reshape(KEY_WINDOW, GROUP_WIDTH)
        vw = vn3_ref[pl.ds(s8, SLAB)].reshape(KEY_WINDOW, GROUP_WIDTH)
        tok = SLAB * (i0 + s8 - 4) + col
        parts = tile(q, kw, vw, 0, (tok >= 0) & (tok < seq_len))
        m1, l1, a1 = widen(parts, 0), widen(parts, 1), widen(parts, 2)
        lse_p = ln_ref[pl.ds(s8, 8)].reshape(Q_BLOCK, GROUP_WIDTH)
        o_p = on_ref[pl.ds(s8, 8)].reshape(Q_BLOCK, GROUP_WIDTH).astype(F32)
        m = jnp.maximum(m1, lse_p)
        alpha, beta = jnp.exp(m1 - m), jnp.exp(lse_p - m)
        out = (a1 * alpha + o_p * beta) / (l1 * alpha + beta)
        o_ref[0, pl.ds(s8, 8)] = out.astype(BF16).reshape(8, SLAB, GROUP_WIDTH)
        return carry

    lax.fori_loop(0, ti // 8, d1_body, 0)


def _attention(qkvg):
    b, l, w4 = qkvg.shape
    s16 = l // SLAB
    ti = min(128, s16)
    hw = HALF_WINDOW
    gw = GROUP_WIDTH
    view = qkvg.reshape(b, s16, SLAB, w4)
    ng = ATTN_WIDTH // gw
    n_halo = s16 // hw
    rq = ti // hw

    def main(part):
        return pl.BlockSpec((1, ti, SLAB, gw), lambda bi, i, g: (bi, i, 0, part * ng + g))

    def prev(part):
        return pl.BlockSpec((1, hw, SLAB, gw),
                            lambda bi, i, g: (bi, jnp.maximum(i * rq - 1, 0), 0, part * ng + g))

    def nxt(part):
        return pl.BlockSpec((1, hw, SLAB, gw),
                            lambda bi, i, g: (bi, jnp.minimum((i + 1) * rq, n_halo - 1), 0, part * ng + g))

    vm = lambda shape, dt: pltpu.VMEM(shape, dt)
    out = pl.pallas_call(
        functools.partial(_attn_kernel, ti=ti, s16=s16, seq_len=l),
        grid=(b, s16 // ti, ng),
        in_specs=[main(0), prev(1), main(1), nxt(1), prev(2), main(2), nxt(2),
                  pl.BlockSpec((3, HEAD_GROUP, Q_BLOCK, KEY_WINDOW), lambda bi, i, g: (0, g, 0, 0))],
        out_specs=pl.BlockSpec((1, ti, SLAB, gw), lambda bi, i, g: (bi, i, 0, g)),
        out_shape=jax.ShapeDtypeStruct((b, s16, SLAB, ATTN_WIDTH), BF16),
        scratch_shapes=[vm((SLAB, ti, gw), BF16), vm((SLAB, ti + 2 * hw, gw), BF16),
                        vm((SLAB, ti + 2 * hw, gw), BF16),
                        vm((ti + 8, SLAB, gw), BF16), vm((ti + 8, SLAB, gw), BF16),
                        vm((SLAB, ti, gw), F32), vm((SLAB, ti, gw), F32), vm((SLAB, ti, gw), F32),
                        vm((ti, SLAB, gw), BF16), vm((ti, SLAB, gw), F32)],
        compiler_params=_cparams("parallel", "parallel", "arbitrary"),
        name="band_attn",
    )(view, view, view, view, view, view, view, jnp.asarray(_band_bias_tables()))
    return out.reshape(b, l, ATTN_WIDTH)


def _filter_kernel(w1t_ref, b1_ref, w2t_ref, b2_ref, w3t_ref, b3_ref, fq_ref, w4_ref, fr_ref, dl_ref,
                   out_ref, *, tn, l):
    hp = lax.Precision.HIGHEST
    c = HYENA_WIDTH
    nbd = FILTER_BANDS
    pos = (pl.program_id(0) * tn + lax.broadcasted_iota(jnp.int32, (1, tn), 1)).astype(F32)
    t = pos * (1.0 / (l - 1))
    ang = fr_ref[...] * ((2.0 * math.pi / l) * pos)
    dot = lambda a, b: jnp.dot(a, b, precision=hp, preferred_element_type=F32)
    h = (w1t_ref[:, 0:1] * t + dot(w1t_ref[:, 1:1 + nbd], jnp.cos(ang))
         - dot(w1t_ref[:, 1 + nbd:1 + 2 * nbd], jnp.sin(ang)))
    h = jnp.sin(fq_ref[:, 0:1] * (h + b1_ref[...]))
    h = jnp.sin(fq_ref[:, 1:2] * (dot(w2t_ref[...], h) + b2_ref[...]))
    h = jnp.sin(fq_ref[:, 2:3] * (dot(w3t_ref[...], h) + b3_ref[...]))
    ht = h.T.astype(BF16)
    n_col = pl.program_id(0) * tn + lax.broadcasted_iota(jnp.int32, (tn, 1), 0)
    decay = jnp.exp(-(n_col.astype(F32) * (1.0 / (l - 1))) * dl_ref[...])
    decay_b = jnp.where(n_col == 0, 0.0, decay)
    for od in range(2 * HYENA_ORDER):
        f = jnp.dot(ht, w4_ref[:, od * c:(od + 1) * c].astype(BF16), preferred_element_type=F32)
        out_ref[od] = (f * (decay_b if od % 2 else decay)).astype(BF16)


def _filters(l, w1, b1, w2, b2, w3, b3, freq, w4):
    tn = min(512, l)
    bands = FILTER_BANDS
    fr = jnp.asarray(np.linspace(1e-4, bands - 1, bands, dtype=np.float32)[:, None])
    max_decay = math.log(DECAY_TARGET) / FAST_DECAY_PCT
    min_decay = math.log(DECAY_TARGET) / SLOW_DECAY_PCT
    deltas = jnp.asarray(np.abs(np.linspace(min_decay, max_decay, HYENA_WIDTH, dtype=np.float32))[None, :])
    full = lambda arr: pl.BlockSpec(arr.shape, lambda i: (0,) * arr.ndim)
    args = (w1.T, b1[:, None], w2.T, b2[:, None], w3.T, b3[:, None], freq.T, w4, fr, deltas)
    return pl.pallas_call(
        functools.partial(_filter_kernel, tn=tn, l=l),
        grid=(l // tn,),
        in_specs=[full(a) for a in args],
        out_specs=pl.BlockSpec((2 * HYENA_ORDER, tn, HYENA_WIDTH), lambda i: (0, i, 0)),
        out_shape=jax.ShapeDtypeStruct((2 * HYENA_ORDER, l, HYENA_WIDTH), BF16),
        compiler_params=_cparams("parallel"),
        name=f"filters_l{l}",
    )(*args)


def _fft_plan(l):
    n = 2 * l
    na = 256 if n >= 32768 else (128 if n >= 8192 else 64)
    return na, n // na


@functools.lru_cache(maxsize=None)
def _dft_tables(l):
    na, nb = _fft_plan(l)
    n = na * nb
    ka = np.arange(na // 2, dtype=np.int64)
    nas = np.arange(na // 2, dtype=np.int64)
    nbs = np.arange(nb, dtype=np.int64)
    num = ((2 * ka[None, :, None] + 1) * (nb * nas[None, None, :] + nbs[:, None, None])) % (2 * n)
    theta = num.astype(np.float64) * (math.pi / n)
    fwd = np.concatenate([np.cos(theta), -np.sin(theta)], axis=1)
    thetat = np.swapaxes(theta, 1, 2)
    inv = (2.0 / n) * np.concatenate([np.cos(thetat), -np.sin(thetat)], axis=2)
    phi = ((nbs[:, None] * nbs[None, :]) % nb).astype(np.float64) * (2.0 * math.pi / nb)
    cc, ss = np.cos(phi), np.sin(phi)
    e2 = np.block([[cc, ss], [-ss, cc]])
    e2i = np.block([[cc, -ss], [ss, cc]])
    cast = lambda arr: np.ascontiguousarray(arr.astype(np.float32))
    return cast(fwd), cast(inv), cast(e2), cast(e2i)


def _s1_kernel(src_ref, m_ref, out_ref):
    xt = jnp.swapaxes(src_ref[0], 0, 1)
    for j in range(SLAB):
        out_ref[0, j] = jnp.dot(m_ref[j], xt[j], preferred_element_type=F32).astype(BF16)


def _stage1(src, mats, *, na, nb):
    bx, l, c = src.shape
    half = na // 2
    return pl.pallas_call(
        _s1_kernel,
        grid=(nb // SLAB, bx),
        in_specs=[pl.BlockSpec((1, half, SLAB, c), lambda j, bi: (bi, 0, j, 0)),
                  pl.BlockSpec((SLAB, na, half), lambda j, bi: (j, 0, 0))],
        out_specs=pl.BlockSpec((1, SLAB, na, c), lambda j, bi: (bi, j, 0, 0)),
        out_shape=jax.ShapeDtypeStruct((bx, nb, na, c), BF16),
        compiler_params=_cparams("parallel", "arbitrary"),
        name=f"dft1_l{l}",
    )(src.reshape(bx, half, nb, c), mats)


def _s2f_kernel(yf_ref, yb_ref, e2_ref, h_ref, *, nb):
    ft = jnp.swapaxes(yf_ref[0].reshape(nb, 2 * SLAB, -1), 0, 1)
    bt = jnp.swapaxes(yb_ref[0].reshape(nb, 2 * SLAB, -1), 0, 1)
    for t in range(SLAB):
        yf = jnp.dot(e2_ref[...], jnp.concatenate([ft[t], ft[SLAB + t]], axis=0), preferred_element_type=F32)
        yb = jnp.dot(e2_ref[...], jnp.concatenate([bt[t], bt[SLAB + t]], axis=0), preferred_element_type=F32)
        h_ref[0, t] = jnp.concatenate([yf[:nb] + yb[:nb], yf[nb:] - yb[nb:]], axis=0).astype(BF16)


def _filter_spectrum(y1, e2, *, na, nb):
    c = HYENA_WIDTH
    y5 = y1.reshape(2 * HYENA_ORDER, nb, 2, na // 2, c)
    spec = lambda d: pl.BlockSpec((1, nb, 2, SLAB, c), lambda o, k: (2 * o + d, 0, 0, k, 0))
    return pl.pallas_call(
        functools.partial(_s2f_kernel, nb=nb),
        grid=(HYENA_ORDER, na // 2 // SLAB),
        in_specs=[spec(0), spec(1), pl.BlockSpec((2 * nb, 2 * nb), lambda o, k: (0, 0))],
        out_specs=pl.BlockSpec((1, SLAB, 2 * nb, c), lambda o, k: (o, k, 0, 0)),
        out_shape=jax.ShapeDtypeStruct((HYENA_ORDER, na // 2, 2 * nb, c), BF16),
        compiler_params=_cparams("parallel", "arbitrary"),
        name=f"filter_spec_n{na * nb}",
    )(y5, y5, e2)


def _s2_kernel(y_ref, h_ref, e2_ref, e2i_ref, out_ref, ot_ref, *, nb):
    yt = jnp.swapaxes(y_ref[0].reshape(nb, 2 * SLAB, -1), 0, 1)
    for t in range(SLAB):
        d = jnp.concatenate([yt[t], yt[SLAB + t]], axis=0)
        y = jnp.dot(e2_ref[...], d, preferred_element_type=F32)
        yr, yi = y[:nb], y[nb:]
        hr, hi = h_ref[0, t, :nb].astype(F32), h_ref[0, t, nb:].astype(F32)
        p = jnp.concatenate([yr * hr - yi * hi, yr * hi + yi * hr], axis=0).astype(BF16)
        o = jnp.dot(e2i_ref[...], p, preferred_element_type=F32)
        ot_ref[t] = o[:nb].astype(BF16)
        ot_ref[SLAB + t] = o[nb:].astype(BF16)
    out_ref[0] = jnp.swapaxes(ot_ref[...], 0, 1).reshape(nb, 2, SLAB, -1)


def _stage2(y1, hspec, order, e2, e2i, *, na, nb):
    bx = y1.shape[0]
    c = HYENA_WIDTH
    y5 = y1.reshape(bx, nb, 2, na // 2, c)
    yspec = pl.BlockSpec((1, nb, 2, SLAB, c), lambda k, bi: (bi, 0, 0, k, 0))
    mat = pl.BlockSpec((2 * nb, 2 * nb), lambda k, bi: (0, 0))
    out = pl.pallas_call(
        functools.partial(_s2_kernel, nb=nb),
        grid=(na // 2 // SLAB, bx),
        in_specs=[yspec, pl.BlockSpec((1, SLAB, 2 * nb, c), lambda k, bi: (order, k, 0, 0)), mat, mat],
        out_specs=yspec,
        out_shape=jax.ShapeDtypeStruct((bx, nb, 2, na // 2, c), BF16),
        scratch_shapes=[pltpu.VMEM((2 * SLAB, nb, c), BF16)],
        compiler_params=_cparams("parallel", "arbitrary"),
        name=f"dft2_n{na * nb}",
    )(y5, hspec, e2, e2i)
    return out.reshape(bx, nb, na, c)


def _s1inv_chain_kernel(y_ref, g_ref, z_ref, gate_ref, d_ref, m_ref, znew_ref, y1_ref):
    zt = jnp.swapaxes(z_ref[0], 0, 1)
    gt = jnp.swapaxes(gate_ref[0], 0, 1)
    for j in range(SLAB):
        conv = jnp.dot(g_ref[j], y_ref[0, j], preferred_element_type=F32)
        zn = (gt[j].astype(F32) * (conv + d_ref[...] * zt[j].astype(F32))).astype(BF16)
        znew_ref[0, j] = zn
        y1_ref[0, j] = jnp.dot(m_ref[j], zn, preferred_element_type=F32).astype(BF16)


def _s1inv_last_kernel(y_ref, g_ref, z_ref, gate_ref, d_ref, znew_ref, zt_ref):
    gt = jnp.swapaxes(gate_ref[0], 0, 1)
    for j in range(SLAB):
        conv = jnp.dot(g_ref[j], y_ref[0, j], preferred_element_type=F32)
        zt_ref[j] = (gt[j].astype(F32) * (conv + d_ref[...] * z_ref[0, j].astype(F32))).astype(BF16)
    znew_ref[0] = jnp.swapaxes(zt_ref[...], 0, 1)


def _stage1_inverse(y3, ginv, z, gate, d, mats, *, na, nb):
    bx = y3.shape[0]
    c = HYENA_WIDTH
    half = na // 2
    l = half * nb
    nat = pl.BlockSpec((1, half, SLAB, c), lambda j, bi: (bi, 0, j, 0))
    tr = lambda rows: pl.BlockSpec((1, SLAB, rows, c), lambda j, bi: (bi, j, 0, 0))
    gspec = pl.BlockSpec((SLAB, half, na), lambda j, bi: (j, 0, 0))
    dspec = pl.BlockSpec((1, c), lambda j, bi: (0, 0))
    gate4 = gate.reshape(bx, half, nb, c)
    if mats is not None:
        return pl.pallas_call(
            _s1inv_chain_kernel,
            grid=(nb // SLAB, bx),
            in_specs=[tr(na), gspec, nat, nat, dspec, pl.BlockSpec((SLAB, na, half), lambda j, bi: (j, 0, 0))],
            out_specs=[tr(half), tr(na)],
            out_shape=[jax.ShapeDtypeStruct((bx, nb, half, c), BF16),
                       jax.ShapeDtypeStruct((bx, nb, na, c), BF16)],
            compiler_params=_cparams("parallel", "arbitrary"),
            name=f"idft1_chain_l{l}",
        )(y3, ginv, z.reshape(bx, half, nb, c), gate4, d, mats)
    out = pl.pallas_call(
        _s1inv_last_kernel,
        grid=(nb // SLAB, bx),
        in_specs=[tr(na), gspec, tr(half), nat, dspec],
        out_specs=nat,
        out_shape=jax.ShapeDtypeStruct((bx, half, nb, c), BF16),
        scratch_shapes=[pltpu.VMEM((SLAB, half, c), BF16)],
        compiler_params=_cparams("parallel", "arbitrary"),
        name=f"idft1_last_l{l}",
    )(y3, ginv, z, gate4, d)
    return out.reshape(bx, l, c)


def _hyena(z0, x1, x2, filt, hyena_d):
    bx, l, c = z0.shape
    na, nb = _fft_plan(l)
    m_fwd, ginv, e2, e2i = (jnp.asarray(t).astype(BF16) for t in _dft_tables(l))
    hspec = _filter_spectrum(_stage1(filt, m_fwd, na=na, nb=nb), e2, na=na, nb=nb)
    y1 = _stage1(z0, m_fwd, na=na, nb=nb)
    y3 = _stage2(y1, hspec, 0, e2, e2i, na=na, nb=nb)
    z1, y1 = _stage1_inverse(y3, ginv, z0, x1, hyena_d[0:1], m_fwd, na=na, nb=nb)
    y3 = _stage2(y1, hspec, 1, e2, e2i, na=na, nb=nb)
    return _stage1_inverse(y3, ginv, z1, x2, hyena_d[1:2], None, na=na, nb=nb)


def _final_kernel(attn_ref, ga_ref, z_ref, gh_ref, x_ref, ag_ref, hg_ref, wo_ref, lg_ref, lb_ref, y_ref):
    def rms(v, g):
        return v * lax.rsqrt(jnp.mean(v * v, axis=-1, keepdims=True) + RMS_EPS) * g

    ma = (rms(attn_ref[0].astype(F32), ag_ref[...]) * ga_ref[0].astype(F32)).astype(BF16)
    mh = (rms(z_ref[0].astype(F32), hg_ref[...]) * gh_ref[0].astype(F32)).astype(BF16)
    a = ATTN_WIDTH
    out = (jnp.dot(ma, wo_ref[0:a, :], preferred_element_type=F32)
           + jnp.dot(mh, wo_ref[a:, :], preferred_element_type=F32))
    h = ((2.0 * DEPTH) ** 0.25) * x_ref[0] + out
    mu = jnp.mean(h, axis=-1, keepdims=True)
    hc = h - mu
    var = jnp.mean(hc * hc, axis=-1, keepdims=True)
    y_ref[0] = hc * lax.rsqrt(var + LN_EPS) * lg_ref[...] + lb_ref[...]


def _final(attn, qkvg, z, gh, x, attn_g, hyena_g, w_out_bf, ln_g, ln_b, *, tt):
    b, l, d = x.shape
    a = ATTN_WIDTH
    tok = lambda w, col=0: pl.BlockSpec((1, tt, w), lambda bi, i: (bi, i, col))
    vec = lambda w: pl.BlockSpec((1, w), lambda bi, i: (0, 0))
    return pl.pallas_call(
        _final_kernel,
        grid=(b, l // tt),
        in_specs=[tok(a), tok(a, 3), tok(a), tok(a), tok(d),
                  vec(a), vec(a), pl.BlockSpec((d, d), lambda bi, i: (0, 0)), vec(d), vec(d)],
        out_specs=tok(d),
        out_shape=jax.ShapeDtypeStruct((b, l, d), F32),
        compiler_params=_cparams("parallel", "arbitrary"),
        name="outproj_ln",
    )(attn, qkvg, z, gh, x, attn_g[None, :], hyena_g[None, :], w_out_bf, ln_g[None, :], ln_b[None, :])


def _layer(x, w_in_bf, conv_w, conv_b, filt_w1, filt_b1, filt_w2, filt_b2, filt_w3, filt_b3,
           filt_freq, filt_w4, hyena_d, attn_norm_g, hyena_norm_g, w_out_bf, ln_g, ln_b):
    b, l, _ = x.shape
    tm = min(512, l)
    qkvg, z0, x1, x2, gh = _inproj(x, w_in_bf, conv_w, conv_b[None, :], tm=tm)
    attn = _attention(qkvg)
    filt = _filters(l, filt_w1, filt_b1, filt_w2, filt_b2, filt_w3, filt_b3, filt_freq, filt_w4)
    z = _hyena(z0, x1, x2, filt, hyena_d)
    return _final(attn, qkvg, z, gh, x, attn_norm_g, hyena_norm_g, w_out_bf, ln_g, ln_b, tt=tm)


def kernel(x_prompt, x_sample, w_in, conv_w, conv_b, filt_w1, filt_b1, filt_w2, filt_b2, filt_w3,
           filt_b3, filt_freq, filt_w4, hyena_d, attn_norm_g, hyena_norm_g, w_out, ln_g, ln_b):
    def trunk(x):
        for i in range(DEPTH):
            x = _layer(x, w_in[i].astype(BF16), conv_w[i], conv_b[i], filt_w1[i], filt_b1[i], filt_w2[i],
                       filt_b2[i], filt_w3[i], filt_b3[i], filt_freq[i], filt_w4[i], hyena_d[i],
                       attn_norm_g[i], hyena_norm_g[i], w_out[i].astype(BF16), ln_g[i], ln_b[i])
        return x
    return (trunk(x_prompt), trunk(x_sample))
```

```python
import functools
import math

import numpy as np
import jax
import jax.numpy as jnp
from jax import lax
from jax.experimental import pallas as pl
from jax.experimental.pallas import tpu as pltpu

F32 = jnp.float32
BF16 = jnp.bfloat16

D_MODEL = 1024
ATTN_WIDTH = 512
HYENA_WIDTH = 512
HEAD_DIM = 64
N_HEADS = 8
HALF_WINDOW = 64
HYENA_ORDER = 2
FILTER_BANDS = 16
FILTER_HIDDEN = 64
FAST_DECAY_PCT = 0.3
SLOW_DECAY_PCT = 1.5
DECAY_TARGET = 1e-2
PROJ_WIDTH = 4096
LN_EPS = 1e-5
RMS_EPS = 1e-6
NEG_INF = -1e30
DEPTH = 1

Q_BLOCK = 128
KEY_WINDOW = 256
SLAB = 16
HEAD_GROUP = 4
GROUP_WIDTH = HEAD_GROUP * HEAD_DIM
LOOP_UNROLL = 8
VMEM_LIMIT = 56 * 1024 * 1024


def _cparams(*sem):
    return pltpu.CompilerParams(dimension_semantics=sem, vmem_limit_bytes=VMEM_LIMIT)


def _silu(g):
    return g / (1.0 + jnp.exp(-g))


def _inproj_kernel(xp_ref, x_ref, xn_ref, w_ref, cw_ref, cb_ref,
                   qkvg_ref, z0_ref, x1_ref, x2_ref, gh_ref, *, tm, n_tiles):
    i = pl.program_id(1)
    a, c = ATTN_WIDTH, HYENA_WIDTH
    xb = x_ref[0].astype(BF16)

    def proj(lo, hi, lhs=xb):
        return jnp.dot(lhs, w_ref[:, lo:hi], preferred_element_type=F32)

    qkvg_ref[0, :, 0:a] = (proj(0, a) * (HEAD_DIM ** -0.5)).astype(BF16)
    qkvg_ref[0, :, a:3 * a] = proj(a, 3 * a).astype(BF16)
    qkvg_ref[0, :, 3 * a:4 * a] = _silu(proj(3 * a, 4 * a)).astype(BF16)
    gh_ref[0] = _silu(proj(4 * a + 3 * c, 4 * a + 4 * c)).astype(BF16)

    u0, u1 = 4 * a, 4 * a + 3 * c
    p = proj(u0, u1)
    prev_row = proj(u0, u1, xp_ref[0, 0].astype(BF16))[7:8, :]
    next_row = proj(u0, u1, xn_ref[0, 0].astype(BF16))[0:1, :]
    prev_row = prev_row * (i > 0).astype(F32)
    next_row = next_row * (i < n_tiles - 1).astype(F32)
    row = lax.broadcasted_iota(jnp.int32, p.shape, 0)
    below = jnp.where(row == 0, prev_row, pltpu.roll(p, 1, 0))
    above = jnp.where(row == tm - 1, next_row, pltpu.roll(p, tm - 1, 0))
    u = cb_ref[...] + below * cw_ref[0:1, :] + p * cw_ref[1:2, :] + above * cw_ref[2:3, :]
    z0_ref[0] = u[:, 0:c].astype(BF16)
    x1_ref[0] = u[:, c:2 * c].astype(BF16)
    x2_ref[0] = u[:, 2 * c:3 * c].astype(BF16)


def _inproj(x, w_bf, conv_w, conv_b, *, tm):
    b, l, d = x.shape
    n_tiles = l // tm
    x8 = x.reshape(b, l // 8, 8, d)
    r = tm // 8
    n8 = l // 8
    bf_out = lambda w: jax.ShapeDtypeStruct((b, l, w), BF16)
    row_spec = lambda w: pl.BlockSpec((1, tm, w), lambda bi, i: (bi, i, 0))
    return pl.pallas_call(
        functools.partial(_inproj_kernel, tm=tm, n_tiles=n_tiles),
        grid=(b, n_tiles),
        in_specs=[
            pl.BlockSpec((1, 1, 8, d), lambda bi, i: (bi, jnp.maximum(i * r - 1, 0), 0, 0)),
            pl.BlockSpec((1, tm, d), lambda bi, i: (bi, i, 0)),
            pl.BlockSpec((1, 1, 8, d), lambda bi, i: (bi, jnp.minimum((i + 1) * r, n8 - 1), 0, 0)),
            pl.BlockSpec((d, PROJ_WIDTH), lambda bi, i: (0, 0)),
            pl.BlockSpec((3, 3 * HYENA_WIDTH), lambda bi, i: (0, 0)),
            pl.BlockSpec((1, 3 * HYENA_WIDTH), lambda bi, i: (0, 0)),
        ],
        out_specs=[row_spec(4 * ATTN_WIDTH), row_spec(HYENA_WIDTH), row_spec(HYENA_WIDTH),
                   row_spec(HYENA_WIDTH), row_spec(HYENA_WIDTH)],
        out_shape=[bf_out(4 * ATTN_WIDTH), bf_out(HYENA_WIDTH), bf_out(HYENA_WIDTH),
                   bf_out(HYENA_WIDTH), bf_out(HYENA_WIDTH)],
        compiler_params=_cparams("parallel", "arbitrary"),
        name="inproj",
    )(x8, x, x8, w_bf, conv_w, conv_b)


def _band_bias_tables():
    qi = np.arange(Q_BLOCK)[:, None]
    kj = np.arange(KEY_WINDOW)[None, :]
    rel_band = kj - HALF_WINDOW - qi
    rel_d4 = 4 * ((kj % 64) - 16 - (qi % 32)) + (kj // 64 - qi // 32)
    slopes = np.asarray([2.0 ** (-8.0 * (i + 1) / N_HEADS) for i in range(N_HEADS)], np.float32)
    out = []
    for rel, dil in ((rel_band, 1), (rel_d4, 4), (rel_band, 16)):
        rel = np.abs(rel)
        bias = -slopes[:, None, None] * (rel * dil).astype(np.float32)[None]
        out.append(np.where(rel[None] <= HALF_WINDOW, bias, NEG_INF).astype(np.float32))
    return np.stack(out)


def _attn_kernel(q_ref, kp_ref, k_ref, kn_ref, vp_ref, v_ref, vn_ref, bias_ref, o_ref,
                 qt_ref, kt_ref, vt_ref, kn3_ref, vn3_ref, acc_ref, m_ref, l_ref, on_ref, ln_ref,
                 *, ti, s16, seq_len):
    i0 = pl.program_id(1) * ti
    hw = HALF_WINDOW
    dn_t = (((1,), (1,)), ((), ()))

    qt_ref[...] = jnp.swapaxes(q_ref[0], 0, 1)
    kt_ref[...] = jnp.swapaxes(jnp.concatenate([kp_ref[0], k_ref[0], kn_ref[0]], axis=0), 0, 1)
    vt_ref[...] = jnp.swapaxes(jnp.concatenate([vp_ref[0], v_ref[0], vn_ref[0]], axis=0), 0, 1)
    kn3_ref[...] = jnp.concatenate([kp_ref[0, hw - 4:hw], k_ref[0], kn_ref[0, 0:4]], axis=0)
    vn3_ref[...] = jnp.concatenate([vp_ref[0, hw - 4:hw], v_ref[0], vn_ref[0, 0:4]], axis=0)

    col = lax.broadcasted_iota(jnp.int32, (1, KEY_WINDOW), 1)

    pair = 2 * HEAD_DIM
    first = lax.broadcasted_iota(jnp.int32, (1, pair), 1) < HEAD_DIM
    ones_cols = jnp.ones((KEY_WINDOW, pair), BF16)

    def tile(q, kw, vw, pat, kvalid):
        ms, ls, accs = [], [], []
        for pr in range(HEAD_GROUP // 2):
            ps = slice(pr * pair, (pr + 1) * pair)
            qp, kp, vext = q[:, ps], kw[:, ps], jnp.concatenate([vw[:, ps], ones_cols], axis=1)
            halves = []
            for hh in range(2):
                own = first if hh == 0 else jnp.logical_not(first)
                s = lax.dot_general(jnp.where(own, qp, jnp.zeros_like(qp)), kp, dn_t,
                                    preferred_element_type=F32)
                s = jnp.where(kvalid, s + bias_ref[pat, 2 * pr + hh], NEG_INF)
                m = jnp.max(s, axis=-1, keepdims=True)
                pv = jnp.dot(jnp.exp(s - m).astype(BF16), vext, preferred_element_type=F32)
                halves.append((m, pv))
            (m0, pv0), (m1, pv1) = halves
            ms.append(jnp.where(first, m0, m1))
            accs.append(jnp.where(first, pv0[:, :pair], pv1[:, :pair]))
            ls.append(jnp.where(first, pv0[:, pair:], pv1[:, pair:]))
        cat = lambda parts: jnp.concatenate(parts, axis=1)
        return cat(ms), cat(ls), cat(accs)

    i16 = i0 - hw + col

    def d16_body(t, carry):
        r = t // (ti // Q_BLOCK)
        js = pl.multiple_of((t % (ti // Q_BLOCK)) * Q_BLOCK, Q_BLOCK)
        m, l, acc = tile(qt_ref[r, pl.ds(js, Q_BLOCK), :], kt_ref[r, pl.ds(js, KEY_WINDOW), :],
                         vt_ref[r, pl.ds(js, KEY_WINDOW), :], 2, (i16 + js >= 0) & (i16 + js < s16))
        m_ref[r, pl.ds(js, Q_BLOCK), :] = m
        l_ref[r, pl.ds(js, Q_BLOCK), :] = l
        acc_ref[r, pl.ds(js, Q_BLOCK), :] = acc
        return carry

    lax.fori_loop(0, SLAB * (ti // Q_BLOCK), d16_body, 0, unroll=LOOP_UNROLL)

    n_sub = ti // 32

    def d4_body(t, carry):
        r4 = t // n_sub
        is_ = pl.multiple_of((t % n_sub) * 32, 32)
        ks = pl.multiple_of(is_ + hw - 16, 16)
        q = jnp.concatenate([qt_ref[r4 + 4 * q4, pl.ds(is_, 32), :] for q4 in range(4)], axis=0)
        kw = jnp.concatenate([kt_ref[r4 + 4 * q4, pl.ds(ks, 64), :] for q4 in range(4)], axis=0)
        vw = jnp.concatenate([vt_ref[r4 + 4 * q4, pl.ds(ks, 64), :] for q4 in range(4)], axis=0)
        i4 = i0 + is_ - 16 + (col % 64)
        m4, l4, a4 = tile(q, kw, vw, 1, (i4 >= 0) & (i4 < s16))
        for q4 in range(4):
            rows = slice(q4 * 32, (q4 + 1) * 32)
            at = (r4 + 4 * q4, pl.ds(is_, 32), slice(None))
            m_old, l_old, a_old = m_ref[at], l_ref[at], acc_ref[at]
            m_new = jnp.maximum(m_old, m4[rows])
            alpha, beta = jnp.exp(m_old - m_new), jnp.exp(m4[rows] - m_new)
            m_ref[at] = m_new
            l_ref[at] = l_old * alpha + l4[rows] * beta
            acc_ref[at] = a_old * alpha + a4[rows] * beta
        return carry

    lax.fori_loop(0, 4 * n_sub, d4_body, 0, unroll=LOOP_UNROLL)

    on_ref[...] = jnp.swapaxes((acc_ref[...] / l_ref[...]).astype(BF16), 0, 1)
    ln_ref[...] = jnp.swapaxes(m_ref[...] + jnp.log(l_ref[...]), 0, 1)

    def d1_body(j, carry):
        s8 = pl.multiple_of(j * 8, 8)
        q = q_ref[0, pl.ds(s8, 8)].reshape(Q_BLOCK, GROUP_WIDTH)
        kw = kn3_ref[pl.ds(s8, SLAB)].reshape(KEY_WINDOW, GROUP_WIDTH)
        vw = vn3_ref[pl.ds(s8, SLAB)].reshape(KEY_WINDOW, GROUP_WIDTH)
        tok = SLAB * (i0 + s8 - 4) + col
        m1, l1, a1 = tile(q, kw, vw, 0, (tok >= 0) & (tok < seq_len))
        lse_p = ln_ref[pl.ds(s8, 8)].reshape(Q_BLOCK, GROUP_WIDTH)
        o_p = on_ref[pl.ds(s8, 8)].reshape(Q_BLOCK, GROUP_WIDTH).astype(F32)
        m = jnp.maximum(m1, lse_p)
        alpha, beta = jnp.exp(m1 - m), jnp.exp(lse_p - m)
        out = (a1 * alpha + o_p * beta) / (l1 * alpha + beta)
        o_ref[0, pl.ds(s8, 8)] = out.astype(BF16).reshape(8, SLAB, GROUP_WIDTH)
        return carry

    lax.fori_loop(0, ti // 8, d1_body, 0, unroll=LOOP_UNROLL)


def _attention(qkvg):
    b, l, w4 = qkvg.shape
    s16 = l // SLAB
    ti = min(128, s16)
    hw = HALF_WINDOW
    gw = GROUP_WIDTH
    view = qkvg.reshape(b, s16, SLAB, w4)
    ng = ATTN_WIDTH // gw
    n_halo = s16 // hw
    rq = ti // hw

    def main(part):
        return pl.BlockSpec((1, ti, SLAB, gw), lambda bi, i, g: (bi, i, 0, part * ng + g))

    def prev(part):
        return pl.BlockSpec((1, hw, SLAB, gw),
                            lambda bi, i, g: (bi, jnp.maximum(i * rq - 1, 0), 0, part * ng + g))

    def nxt(part):
        return pl.BlockSpec((1, hw, SLAB, gw),
                            lambda bi, i, g: (bi, jnp.minimum((i + 1) * rq, n_halo - 1), 0, part * ng + g))

    vm = lambda shape, dt: pltpu.VMEM(shape, dt)
    out = pl.pallas_call(
        functools.partial(_attn_kernel, ti=ti, s16=s16, seq_len=l),
        grid=(b, s16 // ti, ng),
        in_specs=[main(0), prev(1), main(1), nxt(1), prev(2), main(2), nxt(2),
                  pl.BlockSpec((3, HEAD_GROUP, Q_BLOCK, KEY_WINDOW), lambda bi, i, g: (0, g, 0, 0))],
        out_specs=pl.BlockSpec((1, ti, SLAB, gw), lambda bi, i, g: (bi, i, 0, g)),
        out_shape=jax.ShapeDtypeStruct((b, s16, SLAB, ATTN_WIDTH), BF16),
        scratch_shapes=[vm((SLAB, ti, gw), BF16), vm((SLAB, ti + 2 * hw, gw), BF16),
                        vm((SLAB, ti + 2 * hw, gw), BF16),
                        vm((ti + 8, SLAB, gw), BF16), vm((ti + 8, SLAB, gw), BF16),
                        vm((SLAB, ti, gw), F32), vm((SLAB, ti, gw), F32), vm((SLAB, ti, gw), F32),
                        vm((ti, SLAB, gw), BF16), vm((ti, SLAB, gw), F32)],
        compiler_params=_cparams("parallel", "parallel", "arbitrary"),
        name="band_attn",
    )(view, view, view, view, view, view, view, jnp.asarray(_band_bias_tables()))
    return out.reshape(b, l, ATTN_WIDTH)


def _filter_kernel(w1t_ref, b1_ref, w2t_ref, b2_ref, w3t_ref, b3_ref, fq_ref, w4_ref, fr_ref, dl_ref,
                   out_ref, *, tn, l):
    hp = lax.Precision.HIGHEST
    c = HYENA_WIDTH
    nbd = FILTER_BANDS
    pos = (pl.program_id(0) * tn + lax.broadcasted_iota(jnp.int32, (1, tn), 1)).astype(F32)
    t = pos * (1.0 / (l - 1))
    ang = fr_ref[...] * ((2.0 * math.pi / l) * pos)
    dot = lambda a, b: jnp.dot(a, b, precision=hp, preferred_element_type=F32)
    h = (w1t_ref[:, 0:1] * t + dot(w1t_ref[:, 1:1 + nbd], jnp.cos(ang))
         - dot(w1t_ref[:, 1 + nbd:1 + 2 * nbd], jnp.sin(ang)))
    h = jnp.sin(fq_ref[:, 0:1] * (h + b1_ref[...]))
    h = jnp.sin(fq_ref[:, 1:2] * (dot(w2t_ref[...], h) + b2_ref[...]))
    h = jnp.sin(fq_ref[:, 2:3] * (dot(w3t_ref[...], h) + b3_ref[...]))
    ht = h.T.astype(BF16)
    n_col = pl.program_id(0) * tn + lax.broadcasted_iota(jnp.int32, (tn, 1), 0)
    decay = jnp.exp(-(n_col.astype(F32) * (1.0 / (l - 1))) * dl_ref[...])
    decay_b = jnp.where(n_col == 0, 0.0, decay)
    for od in range(2 * HYENA_ORDER):
        f = jnp.dot(ht, w4_ref[:, od * c:(od + 1) * c].astype(BF16), preferred_element_type=F32)
        out_ref[od] = (f * (decay_b if od % 2 else decay)).astype(BF16)


def _filters(l, w1, b1, w2, b2, w3, b3, freq, w4):
    tn = min(512, l)
    bands = FILTER_BANDS
    fr = jnp.asarray(np.linspace(1e-4, bands - 1, bands, dtype=np.float32)[:, None])
    max_decay = math.log(DECAY_TARGET) / FAST_DECAY_PCT
    min_decay = math.log(DECAY_TARGET) / SLOW_DECAY_PCT
    deltas = jnp.asarray(np.abs(np.linspace(min_decay, max_decay, HYENA_WIDTH, dtype=np.float32))[None, :])
    full = lambda arr: pl.BlockSpec(arr.shape, lambda i: (0,) * arr.ndim)
    args = (w1.T, b1[:, None], w2.T, b2[:, None], w3.T, b3[:, None], freq.T, w4, fr, deltas)
    return pl.pallas_call(
        functools.partial(_filter_kernel, tn=tn, l=l),
        grid=(l // tn,),
        in_specs=[full(a) for a in args],
        out_specs=pl.BlockSpec((2 * HYENA_ORDER, tn, HYENA_WIDTH), lambda i: (0, i, 0)),
        out_shape=jax.ShapeDtypeStruct((2 * HYENA_ORDER, l, HYENA_WIDTH), BF16),
        compiler_params=_cparams("parallel"),
        name=f"filters_l{l}",
    )(*args)


def _fft_plan(l):
    n = 2 * l
    na = 256 if n >= 32768 else (128 if n >= 8192 else 64)
    return na, n // na


@functools.lru_cache(maxsize=None)
def _dft_tables(l):
    na, nb = _fft_plan(l)
    n = na * nb
    ka = np.arange(na // 2, dtype=np.int64)
    nas = np.arange(na // 2, dtype=np.int64)
    nbs = np.arange(nb, dtype=np.int64)
    num = ((2 * ka[None, :, None] + 1) * (nb * nas[None, None, :] + nbs[:, None, None])) % (2 * n)
    theta = num.astype(np.float64) * (math.pi / n)
    fwd = np.concatenate([np.cos(theta), -np.sin(theta)], axis=1)
    thetat = np.swapaxes(theta, 1, 2)
    inv = (2.0 / n) * np.concatenate([np.cos(thetat), -np.sin(thetat)], axis=2)
    phi = ((nbs[:, None] * nbs[None, :]) % nb).astype(np.float64) * (2.0 * math.pi / nb)
    cc, ss = np.cos(phi), np.sin(phi)
    e2 = np.block([[cc, ss], [-ss, cc]])
    e2i = np.block([[cc, -ss], [ss, cc]])
    cast = lambda arr: np.ascontiguousarray(arr.astype(np.float32))
    return cast(fwd), cast(inv), cast(e2), cast(e2i)


def _s1_kernel(src_ref, m_ref, out_ref):
    xt = jnp.swapaxes(src_ref[0], 0, 1)
    for j in range(SLAB):
        out_ref[0, j] = jnp.dot(m_ref[j], xt[j], preferred_element_type=F32).astype(BF16)


def _stage1(src, mats, *, na, nb):
    bx, l, c = src.shape
    half = na // 2
    return pl.pallas_call(
        _s1_kernel,
        grid=(nb // SLAB, bx),
        in_specs=[pl.BlockSpec((1, half, SLAB, c), lambda j, bi: (bi, 0, j, 0)),
                  pl.BlockSpec((SLAB, na, half), lambda j, bi: (j, 0, 0))],
        out_specs=pl.BlockSpec((1, SLAB, na, c), lambda j, bi: (bi, j, 0, 0)),
        out_shape=jax.ShapeDtypeStruct((bx, nb, na, c), BF16),
        compiler_params=_cparams("parallel", "arbitrary"),
        name=f"dft1_l{l}",
    )(src.reshape(bx, half, nb, c), mats)


def _s2f_kernel(yf_ref, yb_ref, e2_ref, h_ref, *, nb):
    ft = jnp.swapaxes(yf_ref[0].reshape(nb, 2 * SLAB, -1), 0, 1)
    bt = jnp.swapaxes(yb_ref[0].reshape(nb, 2 * SLAB, -1), 0, 1)
    for t in range(SLAB):
        yf = jnp.dot(e2_ref[...], jnp.concatenate([ft[t], ft[SLAB + t]], axis=0), preferred_element_type=F32)
        yb = jnp.dot(e2_ref[...], jnp.concatenate([bt[t], bt[SLAB + t]], axis=0), preferred_element_type=F32)
        h_ref[0, t] = jnp.concatenate([yf[:nb] + yb[:nb], yf[nb:] - yb[nb:]], axis=0).astype(BF16)


def _filter_spectrum(y1, e2, *, na, nb):
    c = HYENA_WIDTH
    y5 = y1.reshape(2 * HYENA_ORDER, nb, 2, na // 2, c)
    spec = lambda d: pl.BlockSpec((1, nb, 2, SLAB, c), lambda o, k: (2 * o + d, 0, 0, k, 0))
    return pl.pallas_call(
        functools.partial(_s2f_kernel, nb=nb),
        grid=(HYENA_ORDER, na // 2 // SLAB),
        in_specs=[spec(0), spec(1), pl.BlockSpec((2 * nb, 2 * nb), lambda o, k: (0, 0))],
        out_specs=pl.BlockSpec((1, SLAB, 2 * nb, c), lambda o, k: (o, k, 0, 0)),
        out_shape=jax.ShapeDtypeStruct((HYENA_ORDER, na // 2, 2 * nb, c), BF16),
        compiler_params=_cparams("parallel", "arbitrary"),
        name=f"filter_spec_n{na * nb}",
    )(y5, y5, e2)


def _s2_kernel(y_ref, h_ref, e2_ref, e2i_ref, out_ref, ot_ref, *, nb):
    yt = jnp.swapaxes(y_ref[0].reshape(nb, 2 * SLAB, -1), 0, 1)
    for t in range(SLAB):
        d = jnp.concatenate([yt[t], yt[SLAB + t]], axis=0)
        y = jnp.dot(e2_ref[...], d, preferred_element_type=F32)
        yr, yi = y[:nb], y[nb:]
        hr, hi = h_ref[0, t, :nb].astype(F32), h_ref[0, t, nb:].astype(F32)
        p = jnp.concatenate([yr * hr - yi * hi, yr * hi + yi * hr], axis=0).astype(BF16)
        o = jnp.dot(e2i_ref[...], p, preferred_element_type=F32)
        ot_ref[t] = o[:nb].astype(BF16)
        ot_ref[SLAB + t] = o[nb:].astype(BF16)
    out_ref[0] = jnp.swapaxes(ot_ref[...], 0, 1).reshape(nb, 2, SLAB, -1)


def _stage2(y1, hspec, order, e2, e2i, *, na, nb):
    bx = y1.shape[0]
    c = HYENA_WIDTH
    y5 = y1.reshape(bx, nb, 2, na // 2, c)
    yspec = pl.BlockSpec((1, nb, 2, SLAB, c), lambda k, bi: (bi, 0, 0, k, 0))
    mat = pl.BlockSpec((2 * nb, 2 * nb), lambda k, bi: (0, 0))
    out = pl.pallas_call(
        functools.partial(_s2_kernel, nb=nb),
        grid=(na // 2 // SLAB, bx),
        in_specs=[yspec, pl.BlockSpec((1, SLAB, 2 * nb, c), lambda k, bi: (order, k, 0, 0)), mat, mat],
        out_specs=yspec,
        out_shape=jax.ShapeDtypeStruct((bx, nb, 2, na // 2, c), BF16),
        scratch_shapes=[pltpu.VMEM((2 * SLAB, nb, c), BF16)],
        compiler_params=_cparams("parallel", "arbitrary"),
        name=f"dft2_n{na * nb}",
    )(y5, hspec, e2, e2i)
    return out.reshape(bx, nb, na, c)


def _s1inv_chain_kernel(y_ref, g_ref, z_ref, gate_ref, d_ref, m_ref, znew_ref, y1_ref):
    zt = jnp.swapaxes(z_ref[0], 0, 1)
    gt = jnp.swapaxes(gate_ref[0], 0, 1)
    for j in range(SLAB):
        conv = jnp.dot(g_ref[j], y_ref[0, j], preferred_element_type=F32)
        zn = (gt[j].astype(F32) * (conv + d_ref[...] * zt[j].astype(F32))).astype(BF16)
        znew_ref[0, j] = zn
        y1_ref[0, j] = jnp.dot(m_ref[j], zn, preferred_element_type=F32).astype(BF16)


def _s1inv_last_kernel(y_ref, g_ref, z_ref, gate_ref, d_ref, znew_ref, zt_ref):
    gt = jnp.swapaxes(gate_ref[0], 0, 1)
    for j in range(SLAB):
        conv = jnp.dot(g_ref[j], y_ref[0, j], preferred_element_type=F32)
        zt_ref[j] = (gt[j].astype(F32) * (conv + d_ref[...] * z_ref[0, j].astype(F32))).astype(BF16)
    znew_ref[0] = jnp.swapaxes(zt_ref[...], 0, 1)


def _stage1_inverse(y3, ginv, z, gate, d, mats, *, na, nb):
    bx = y3.shape[0]
    c = HYENA_WIDTH
    half = na // 2
    l = half * nb
    nat = pl.BlockSpec((1, half, SLAB, c), lambda j, bi: (bi, 0, j, 0))
    tr = lambda rows: pl.BlockSpec((1, SLAB, rows, c), lambda j, bi: (bi, j, 0, 0))
    gspec = pl.BlockSpec((SLAB, half, na), lambda j, bi: (j, 0, 0))
    dspec = pl.BlockSpec((1, c), lambda j, bi: (0, 0))
    gate4 = gate.reshape(bx, half, nb, c)
    if mats is not None:
        return pl.pallas_call(
            _s1inv_chain_kernel,
            grid=(nb // SLAB, bx),
            in_specs=[tr(na), gspec, nat, nat, dspec, pl.BlockSpec((SLAB, na, half), lambda j, bi: (j, 0, 0))],
            out_specs=[tr(half), tr(na)],
            out_shape=[jax.ShapeDtypeStruct((bx, nb, half, c), BF16),
                       jax.ShapeDtypeStruct((bx, nb, na, c), BF16)],
            compiler_params=_cparams("parallel", "arbitrary"),
            name=f"idft1_chain_l{l}",
        )(y3, ginv, z.reshape(bx, half, nb, c), gate4, d, mats)
    out = pl.pallas_call(
        _s1inv_last_kernel,
        grid=(nb // SLAB, bx),
        in_specs=[tr(na), gspec, tr(half), nat, dspec],
        out_specs=nat,
        out_shape=jax.ShapeDtypeStruct((bx, half, nb, c), BF16),
        scratch_shapes=[pltpu.VMEM((SLAB, half, c), BF16)],
        compiler_params=_cparams("parallel", "arbitrary"),
        name=f"idft1_last_l{l}",
    )(y3, ginv, z, gate4, d)
    return out.reshape(bx, l, c)


def _hyena(z0, x1, x2, filt, hyena_d):
    bx, l, c = z0.shape
    na, nb = _fft_plan(l)
    m_fwd, ginv, e2, e2i = (jnp.asarray(t).astype(BF16) for t in _dft_tables(l))
    hspec = _filter_spectrum(_stage1(filt, m_fwd, na=na, nb=nb), e2, na=na, nb=nb)
    y1 = _stage1(z0, m_fwd, na=na, nb=nb)
    y3 = _stage2(y1, hspec, 0, e2, e2i, na=na, nb=nb)
    z1, y1 = _stage1_inverse(y3, ginv, z0, x1, hyena_d[0:1], m_fwd, na=na, nb=nb)
    y3 = _stage2(y1, hspec, 1, e2, e2i, na=na, nb=nb)
    return _stage1_inverse(y3, ginv, z1, x2, hyena_d[1:2], None, na=na, nb=nb)


def _final_kernel(attn_ref, ga_ref, z_ref, gh_ref, x_ref, ag_ref, hg_ref, wo_ref, lg_ref, lb_ref, y_ref):
    def rms(v, g):
        return v * lax.rsqrt(jnp.mean(v * v, axis=-1, keepdims=True) + RMS_EPS) * g

    ma = (rms(attn_ref[0].astype(F32), ag_ref[...]) * ga_ref[0].astype(F32)).astype(BF16)
    mh = (rms(z_ref[0].astype(F32), hg_ref[...]) * gh_ref[0].astype(F32)).astype(BF16)
    a = ATTN_WIDTH
    out = (jnp.dot(ma, wo_ref[0:a, :], preferred_element_type=F32)
           + jnp.dot(mh, wo_ref[a:, :], preferred_element_type=F32))
    h = ((2.0 * DEPTH) ** 0.25) * x_ref[0] + out
    mu = jnp.mean(h, axis=-1, keepdims=True)
    hc = h - mu
    var = jnp.mean(hc * hc, axis=-1, keepdims=True)
    y_ref[0] = hc * lax.rsqrt(var + LN_EPS) * lg_ref[...] + lb_ref[...]


def _final(attn, qkvg, z, gh, x, attn_g, hyena_g, w_out_bf, ln_g, ln_b, *, tt):
    b, l, d = x.shape
    a = ATTN_WIDTH
    tok = lambda w, col=0: pl.BlockSpec((1, tt, w), lambda bi, i: (bi, i, col))
    vec = lambda w: pl.BlockSpec((1, w), lambda bi, i: (0, 0))
    return pl.pallas_call(
        _final_kernel,
        grid=(b, l // tt),
        in_specs=[tok(a), tok(a, 3), tok(a), tok(a), tok(d),
                  vec(a), vec(a), pl.BlockSpec((d, d), lambda bi, i: (0, 0)), vec(d), vec(d)],
        out_specs=tok(d),
        out_shape=jax.ShapeDtypeStruct((b, l, d), F32),
        compiler_params=_cparams("parallel", "arbitrary"),
        name="outproj_ln",
    )(attn, qkvg, z, gh, x, attn_g[None, :], hyena_g[None, :], w_out_bf, ln_g[None, :], ln_b[None, :])


def _layer(x, w_in_bf, conv_w, conv_b, filt_w1, filt_b1, filt_w2, filt_b2, filt_w3, filt_b3,
           filt_freq, filt_w4, hyena_d, attn_norm_g, hyena_norm_g, w_out_bf, ln_g, ln_b):
    b, l, _ = x.shape
    tm = min(512, l)
    qkvg, z0, x1, x2, gh = _inproj(x, w_in_bf, conv_w, conv_b[None, :], tm=tm)
    attn = _attention(qkvg)
    filt = _filters(l, filt_w1, filt_b1, filt_w2, filt_b2, filt_w3, filt_b3, filt_freq, filt_w4)
    z = _hyena(z0, x1, x2, filt, hyena_d)
    return _final(attn, qkvg, z, gh, x, attn_norm_g, hyena_norm_g, w_out_bf, ln_g, ln_b, tt=tm)


def kernel(x_prompt, x_sample, w_in, conv_w, conv_b, filt_w1, filt_b1, filt_w2, filt_b2, filt_w3,
           filt_b3, filt_freq, filt_w4, hyena_d, attn_norm_g, hyena_norm_g, w_out, ln_g, ln_b):
    def trunk(x):
        for i in range(DEPTH):
            x = _layer(x, w_in[i].astype(BF16), conv_w[i], conv_b[i], filt_w1[i], filt_b1[i], filt_w2[i],
                       filt_b2[i], filt_w3[i], filt_b3[i], filt_freq[i], filt_w4[i], hyena_d[i],
                       attn_norm_g[i], hyena_norm_g[i], w_out[i].astype(BF16), ln_g[i], ln_b[i])
        return x
    return (trunk(x_prompt), trunk(x_sample))
```

```python
import functools
import math

import numpy as np
import jax
import jax.numpy as jnp
from jax import lax
from jax.experimental import pallas as pl
from jax.experimental.pallas import tpu as pltpu

F32 = jnp.float32
BF16 = jnp.bfloat16

D_MODEL = 1024
ATTN_WIDTH = 512
HYENA_WIDTH = 512
HEAD_DIM = 64
N_HEADS = 8
HALF_WINDOW = 64
HYENA_ORDER = 2
FILTER_BANDS = 16
FILTER_HIDDEN = 64
FAST_DECAY_PCT = 0.3
SLOW_DECAY_PCT = 1.5
DECAY_TARGET = 1e-2
PROJ_WIDTH = 4096
LN_EPS = 1e-5
RMS_EPS = 1e-6
NEG_INF = -1e30
DEPTH = 1

Q_BLOCK = 128
KEY_WINDOW = 256
SLAB = 16
HEAD_GROUP = 4
GROUP_WIDTH = HEAD_GROUP * HEAD_DIM
LOOP_UNROLL = 8
VMEM_LIMIT = 56 * 1024 * 1024


def _cparams(*sem):
    return pltpu.CompilerParams(dimension_semantics=sem, vmem_limit_bytes=VMEM_LIMIT)


def _silu(g):
    return g / (1.0 + jnp.exp(-g))


def _inproj_kernel(xp_ref, x_ref, xn_ref, w_ref, cw_ref, cb_ref,
                   qkvg_ref, z0_ref, x1_ref, x2_ref, gh_ref, *, tm, n_tiles):
    i = pl.program_id(1)
    a, c = ATTN_WIDTH, HYENA_WIDTH
    xb = x_ref[0].astype(BF16)

    def proj(lo, hi, lhs=xb):
        return jnp.dot(lhs, w_ref[:, lo:hi], preferred_element_type=F32)

    qkvg_ref[0, :, 0:a] = (proj(0, a) * (HEAD_DIM ** -0.5)).astype(BF16)
    qkvg_ref[0, :, a:3 * a] = proj(a, 3 * a).astype(BF16)
    qkvg_ref[0, :, 3 * a:4 * a] = _silu(proj(3 * a, 4 * a)).astype(BF16)
    gh_ref[0] = _silu(proj(4 * a + 3 * c, 4 * a + 4 * c)).astype(BF16)

    u0, u1 = 4 * a, 4 * a + 3 * c
    halo = jnp.concatenate([xp_ref[0, 0], xn_ref[0, 0]], axis=0).astype(BF16)
    pext = proj(u0, u1, jnp.concatenate([xb, halo], axis=0))
    p = pext[:tm]
    prev_row = pext[tm + 7:tm + 8] * (i > 0).astype(F32)
    next_row = pext[tm + 8:tm + 9] * (i < n_tiles - 1).astype(F32)
    row = lax.broadcasted_iota(jnp.int32, p.shape, 0)
    below = jnp.where(row == 0, prev_row, pltpu.roll(p, 1, 0))
    above = jnp.where(row == tm - 1, next_row, pltpu.roll(p, tm - 1, 0))
    u = cb_ref[...] + below * cw_ref[0:1, :] + p * cw_ref[1:2, :] + above * cw_ref[2:3, :]
    z0_ref[0] = u[:, 0:c].astype(BF16)
    x1_ref[0] = u[:, c:2 * c].astype(BF16)
    x2_ref[0] = u[:, 2 * c:3 * c].astype(BF16)


def _inproj(x, w_bf, conv_w, conv_b, *, tm):
    b, l, d = x.shape
    n_tiles = l // tm
    x8 = x.reshape(b, l // 8, 8, d)
    r = tm // 8
    n8 = l // 8
    bf_out = lambda w: jax.ShapeDtypeStruct((b, l, w), BF16)
    row_spec = lambda w: pl.BlockSpec((1, tm, w), lambda bi, i: (bi, i, 0))
    return pl.pallas_call(
        functools.partial(_inproj_kernel, tm=tm, n_tiles=n_tiles),
        grid=(b, n_tiles),
        in_specs=[
            pl.BlockSpec((1, 1, 8, d), lambda bi, i: (bi, jnp.maximum(i * r - 1, 0), 0, 0)),
            pl.BlockSpec((1, tm, d), lambda bi, i: (bi, i, 0)),
            pl.BlockSpec((1, 1, 8, d), lambda bi, i: (bi, jnp.minimum((i + 1) * r, n8 - 1), 0, 0)),
            pl.BlockSpec((d, PROJ_WIDTH), lambda bi, i: (0, 0)),
            pl.BlockSpec((3, 3 * HYENA_WIDTH), lambda bi, i: (0, 0)),
            pl.BlockSpec((1, 3 * HYENA_WIDTH), lambda bi, i: (0, 0)),
        ],
        out_specs=[row_spec(4 * ATTN_WIDTH), row_spec(HYENA_WIDTH), row_spec(HYENA_WIDTH),
                   row_spec(HYENA_WIDTH), row_spec(HYENA_WIDTH)],
        out_shape=[bf_out(4 * ATTN_WIDTH), bf_out(HYENA_WIDTH), bf_out(HYENA_WIDTH),
                   bf_out(HYENA_WIDTH), bf_out(HYENA_WIDTH)],
        compiler_params=_cparams("parallel", "arbitrary"),
        name="inproj",
    )(x8, x, x8, w_bf, conv_w, conv_b)


def _band_bias_tables():
    qi = np.arange(Q_BLOCK)[:, None]
    kj = np.arange(KEY_WINDOW)[None, :]
    rel_band = kj - HALF_WINDOW - qi
    rel_d4 = 4 * ((kj % 64) - 16 - (qi % 32)) + (kj // 64 - qi // 32)
    slopes = np.asarray([2.0 ** (-8.0 * (i + 1) / N_HEADS) for i in range(N_HEADS)], np.float32)
    out = []
    for rel, dil in ((rel_band, 1), (rel_d4, 4), (rel_band, 16)):
        rel = np.abs(rel)
        bias = -slopes[:, None, None] * (rel * dil).astype(np.float32)[None]
        out.append(np.where(rel[None] <= HALF_WINDOW, bias, NEG_INF).astype(np.float32))
    return np.stack(out)


def _attn_kernel(q_ref, kp_ref, k_ref, kn_ref, vp_ref, v_ref, vn_ref, bias_ref, o_ref,
                 qt_ref, kt_ref, vt_ref, kn3_ref, vn3_ref, acc_ref, m_ref, l_ref, on_ref, ln_ref,
                 *, ti, s16, seq_len):
    i0 = pl.program_id(1) * ti
    hw = HALF_WINDOW
    dn_t = (((1,), (1,)), ((), ()))

    qt_ref[...] = jnp.swapaxes(q_ref[0], 0, 1)
    kt_ref[...] = jnp.swapaxes(jnp.concatenate([kp_ref[0], k_ref[0], kn_ref[0]], axis=0), 0, 1)
    vt_ref[...] = jnp.swapaxes(jnp.concatenate([vp_ref[0], v_ref[0], vn_ref[0]], axis=0), 0, 1)
    kn3_ref[...] = jnp.concatenate([kp_ref[0, hw - 4:hw], k_ref[0], kn_ref[0, 0:4]], axis=0)
    vn3_ref[...] = jnp.concatenate([vp_ref[0, hw - 4:hw], v_ref[0], vn_ref[0, 0:4]], axis=0)

    col = lax.broadcasted_iota(jnp.int32, (1, KEY_WINDOW), 1)

    pair = 2 * HEAD_DIM
    first = lax.broadcasted_iota(jnp.int32, (1, pair), 1) < HEAD_DIM
    ones_cols = jnp.ones((KEY_WINDOW, pair), BF16)

    def tile(q, kw, vw, pat, kvalid):
        ms, ls, accs = [], [], []
        for pr in range(HEAD_GROUP // 2):
            ps = slice(pr * pair, (pr + 1) * pair)
            qp, kp, vext = q[:, ps], kw[:, ps], jnp.concatenate([vw[:, ps], ones_cols], axis=1)
            halves = []
            for hh in range(2):
                own = first if hh == 0 else jnp.logical_not(first)
                s = lax.dot_general(jnp.where(own, qp, jnp.zeros_like(qp)), kp, dn_t,
                                    preferred_element_type=F32)
                s = jnp.where(kvalid, s + bias_ref[pat, 2 * pr + hh], NEG_INF)
                m = jnp.max(s, axis=-1, keepdims=True)
                pv = jnp.dot(jnp.exp(s - m).astype(BF16), vext, preferred_element_type=F32)
                halves.append((m, pv))
            (m0, pv0), (m1, pv1) = halves
            ms.append(jnp.where(first, m0, m1))
            accs.append(jnp.where(first, pv0[:, :pair], pv1[:, :pair]))
            ls.append(jnp.where(first, pv0[:, pair:], pv1[:, pair:]))
        cat = lambda parts: jnp.concatenate(parts, axis=1)
        return cat(ms), cat(ls), cat(accs)

    i16 = i0 - hw + col

    def d16_body(t, carry):
        r = t // (ti // Q_BLOCK)
        js = pl.multiple_of((t % (ti // Q_BLOCK)) * Q_BLOCK, Q_BLOCK)
        m, l, acc = tile(qt_ref[r, pl.ds(js, Q_BLOCK), :], kt_ref[r, pl.ds(js, KEY_WINDOW), :],
                         vt_ref[r, pl.ds(js, KEY_WINDOW), :], 2, (i16 + js >= 0) & (i16 + js < s16))
        m_ref[r, pl.ds(js, Q_BLOCK), :] = m
        l_ref[r, pl.ds(js, Q_BLOCK), :] = l
        acc_ref[r, pl.ds(js, Q_BLOCK), :] = acc
        return carry

    lax.fori_loop(0, SLAB * (ti // Q_BLOCK), d16_body, 0, unroll=LOOP_UNROLL)

    n_sub = ti // 32

    def d4_body(t, carry):
        r4 = t // n_sub
        is_ = pl.multiple_of((t % n_sub) * 32, 32)
        ks = pl.multiple_of(is_ + hw - 16, 16)
        q = jnp.concatenate([qt_ref[r4 + 4 * q4, pl.ds(is_, 32), :] for q4 in range(4)], axis=0)
        kw = jnp.concatenate([kt_ref[r4 + 4 * q4, pl.ds(ks, 64), :] for q4 in range(4)], axis=0)
        vw = jnp.concatenate([vt_ref[r4 + 4 * q4, pl.ds(ks, 64), :] for q4 in range(4)], axis=0)
        i4 = i0 + is_ - 16 + (col % 64)
        m4, l4, a4 = tile(q, kw, vw, 1, (i4 >= 0) & (i4 < s16))
        for q4 in range(4):
            rows = slice(q4 * 32, (q4 + 1) * 32)
            at = (r4 + 4 * q4, pl.ds(is_, 32), slice(None))
            m_old, l_old, a_old = m_ref[at], l_ref[at], acc_ref[at]
            m_new = jnp.maximum(m_old, m4[rows])
            alpha, beta = jnp.exp(m_old - m_new), jnp.exp(m4[rows] - m_new)
            m_ref[at] = m_new
            l_ref[at] = l_old * alpha + l4[rows] * beta
            acc_ref[at] = a_old * alpha + a4[rows] * beta
        return carry

    lax.fori_loop(0, 4 * n_sub, d4_body, 0, unroll=LOOP_UNROLL)

    on_ref[...] = jnp.swapaxes((acc_ref[...] / l_ref[...]).astype(BF16), 0, 1)
    ln_ref[...] = jnp.swapaxes(m_ref[...] + jnp.log(l_ref[...]), 0, 1)

    def d1_body(j, carry):
        s8 = pl.multiple_of(j * 8, 8)
        q = q_ref[0, pl.ds(s8, 8)].reshape(Q_BLOCK, GROUP_WIDTH)
        kw = kn3_ref[pl.ds(s8, SLAB)].reshape(KEY_WINDOW, GROUP_WIDTH)
        vw = vn3_ref[pl.ds(s8, SLAB)].reshape(KEY_WINDOW, GROUP_WIDTH)
        tok = SLAB * (i0 + s8 - 4) + col
        m1, l1, a1 = tile(q, kw, vw, 0, (tok >= 0) & (tok < seq_len))
        lse_p = ln_ref[pl.ds(s8, 8)].reshape(Q_BLOCK, GROUP_WIDTH)
        o_p = on_ref[pl.ds(s8, 8)].reshape(Q_BLOCK, GROUP_WIDTH).astype(F32)
        m = jnp.maximum(m1, lse_p)
        alpha, beta = jnp.exp(m1 - m), jnp.exp(lse_p - m)
        out = (a1 * alpha + o_p * beta) / (l1 * alpha + beta)
        o_ref[0, pl.ds(s8, 8)] = out.astype(BF16).reshape(8, SLAB, GROUP_WIDTH)
        return carry

    lax.fori_loop(0, ti // 8, d1_body, 0, unroll=LOOP_UNROLL)


def _attention(qkvg):
    b, l, w4 = qkvg.shape
    s16 = l // SLAB
    ti = min(128, s16)
    hw = HALF_WINDOW
    gw = GROUP_WIDTH
    view = qkvg.reshape(b, s16, SLAB, w4)
    ng = ATTN_WIDTH // gw
    n_halo = s16 // hw
    rq = ti // hw

    def main(part):
        return pl.BlockSpec((1, ti, SLAB, gw), lambda bi, i, g: (bi, i, 0, part * ng + g))

    def prev(part):
        return pl.BlockSpec((1, hw, SLAB, gw),
                            lambda bi, i, g: (bi, jnp.maximum(i * rq - 1, 0), 0, part * ng + g))

    def nxt(part):
        return pl.BlockSpec((1, hw, SLAB, gw),
                            lambda bi, i, g: (bi, jnp.minimum((i + 1) * rq, n_halo - 1), 0, part * ng + g))

    vm = lambda shape, dt: pltpu.VMEM(shape, dt)
    out = pl.pallas_call(
        functools.partial(_attn_kernel, ti=ti, s16=s16, seq_len=l),
        grid=(b, s16 // ti, ng),
        in_specs=[main(0), prev(1), main(1), nxt(1), prev(2), main(2), nxt(2),
                  pl.BlockSpec((3, HEAD_GROUP, Q_BLOCK, KEY_WINDOW), lambda bi, i, g: (0, g, 0, 0))],
        out_specs=pl.BlockSpec((1, ti, SLAB, gw), lambda bi, i, g: (bi, i, 0, g)),
        out_shape=jax.ShapeDtypeStruct((b, s16, SLAB, ATTN_WIDTH), BF16),
        scratch_shapes=[vm((SLAB, ti, gw), BF16), vm((SLAB, ti + 2 * hw, gw), BF16),
                        vm((SLAB, ti + 2 * hw, gw), BF16),
                        vm((ti + 8, SLAB, gw), BF16), vm((ti + 8, SLAB, gw), BF16),
                        vm((SLAB, ti, gw), F32), vm((SLAB, ti, gw), F32), vm((SLAB, ti, gw), F32),
                        vm((ti, SLAB, gw), BF16), vm((ti, SLAB, gw), F32)],
        compiler_params=_cparams("parallel", "parallel", "arbitrary"),
        name="band_attn",
    )(view, view, view, view, view, view, view, jnp.asarray(_band_bias_tables()))
    return out.reshape(b, l, ATTN_WIDTH)


def _filter_kernel(w1t_ref, b1_ref, w2t_ref, b2_ref, w3t_ref, b3_ref, fq_ref, w4_ref, fr_ref, dl_ref,
                   out_ref, *, tn, l):
    hp = lax.Precision.HIGHEST
    c = HYENA_WIDTH
    nbd = FILTER_BANDS
    pos = (pl.program_id(0) * tn + lax.broadcasted_iota(jnp.int32, (1, tn), 1)).astype(F32)
    t = pos * (1.0 / (l - 1))
    ang = fr_ref[...] * ((2.0 * math.pi / l) * pos)
    dot = lambda a, b: jnp.dot(a, b, precision=hp, preferred_element_type=F32)
    h = (w1t_ref[:, 0:1] * t + dot(w1t_ref[:, 1:1 + nbd], jnp.cos(ang))
         - dot(w1t_ref[:, 1 + nbd:1 + 2 * nbd], jnp.sin(ang)))
    h = jnp.sin(fq_ref[:, 0:1] * (h + b1_ref[...]))
    h = jnp.sin(fq_ref[:, 1:2] * (dot(w2t_ref[...], h) + b2_ref[...]))
    h = jnp.sin(fq_ref[:, 2:3] * (dot(w3t_ref[...], h) + b3_ref[...]))
    ht = h.T.astype(BF16)
    n_col = pl.program_id(0) * tn + lax.broadcasted_iota(jnp.int32, (tn, 1), 0)
    decay = jnp.exp(-(n_col.astype(F32) * (1.0 / (l - 1))) * dl_ref[...])
    decay_b = jnp.where(n_col == 0, 0.0, decay)
    for od in range(2 * HYENA_ORDER):
        f = jnp.dot(ht, w4_ref[:, od * c:(od + 1) * c].astype(BF16), preferred_element_type=F32)
        out_ref[od] = (f * (decay_b if od % 2 else decay)).astype(BF16)


def _filters(l, w1, b1, w2, b2, w3, b3, freq, w4):
    tn = min(512, l)
    bands = FILTER_BANDS
    fr = jnp.asarray(np.linspace(1e-4, bands - 1, bands, dtype=np.float32)[:, None])
    max_decay = math.log(DECAY_TARGET) / FAST_DECAY_PCT
    min_decay = math.log(DECAY_TARGET) / SLOW_DECAY_PCT
    deltas = jnp.asarray(np.abs(np.linspace(min_decay, max_decay, HYENA_WIDTH, dtype=np.float32))[None, :])
    full = lambda arr: pl.BlockSpec(arr.shape, lambda i: (0,) * arr.ndim)
    args = (w1.T, b1[:, None], w2.T, b2[:, None], w3.T, b3[:, None], freq.T, w4, fr, deltas)
    return pl.pallas_call(
        functools.partial(_filter_kernel, tn=tn, l=l),
        grid=(l // tn,),
        in_specs=[full(a) for a in args],
        out_specs=pl.BlockSpec((2 * HYENA_ORDER, tn, HYENA_WIDTH), lambda i: (0, i, 0)),
        out_shape=jax.ShapeDtypeStruct((2 * HYENA_ORDER, l, HYENA_WIDTH), BF16),
        compiler_params=_cparams("parallel"),
        name=f"filters_l{l}",
    )(*args)


def _fft_plan(l):
    n = 2 * l
    na = 256 if n >= 32768 else (128 if n >= 8192 else 64)
    return na, n // na


@functools.lru_cache(maxsize=None)
def _dft_tables(l):
    na, nb = _fft_plan(l)
    n = na * nb
    ka = np.arange(na // 2, dtype=np.int64)
    nas = np.arange(na // 2, dtype=np.int64)
    nbs = np.arange(nb, dtype=np.int64)
    num = ((2 * ka[None, :, None] + 1) * (nb * nas[None, None, :] + nbs[:, None, None])) % (2 * n)
    theta = num.astype(np.float64) * (math.pi / n)
    fwd = np.concatenate([np.cos(theta), -np.sin(theta)], axis=1)
    thetat = np.swapaxes(theta, 1, 2)
    inv = (2.0 / n) * np.concatenate([np.cos(thetat), -np.sin(thetat)], axis=2)
    phi = ((nbs[:, None] * nbs[None, :]) % nb).astype(np.float64) * (2.0 * math.pi / nb)
    cc, ss = np.cos(phi), np.sin(phi)
    e2 = np.block([[cc, ss], [-ss, cc]])
    e2i = np.block([[cc, -ss], [ss, cc]])
    cast = lambda arr: np.ascontiguousarray(arr.astype(np.float32))
    return cast(fwd), cast(inv), cast(e2), cast(e2i)


def _s1_kernel(src_ref, m_ref, out_ref):
    xt = jnp.swapaxes(src_ref[0], 0, 1)
    for j in range(SLAB):
        out_ref[0, j] = jnp.dot(m_ref[j], xt[j], preferred_element_type=F32).astype(BF16)


def _stage1(src, mats, *, na, nb):
    bx, l, c = src.shape
    half = na // 2
    return pl.pallas_call(
        _s1_kernel,
        grid=(nb // SLAB, bx),
        in_specs=[pl.BlockSpec((1, half, SLAB, c), lambda j, bi: (bi, 0, j, 0)),
                  pl.BlockSpec((SLAB, na, half), lambda j, bi: (j, 0, 0))],
        out_specs=pl.BlockSpec((1, SLAB, na, c), lambda j, bi: (bi, j, 0, 0)),
        out_shape=jax.ShapeDtypeStruct((bx, nb, na, c), BF16),
        compiler_params=_cparams("parallel", "arbitrary"),
        name=f"dft1_l{l}",
    )(src.reshape(bx, half, nb, c), mats)


def _s2f_kernel(yf_ref, yb_ref, e2_ref, h_ref, *, nb):
    ft = jnp.swapaxes(yf_ref[0].reshape(nb, 2 * SLAB, -1), 0, 1)
    bt = jnp.swapaxes(yb_ref[0].reshape(nb, 2 * SLAB, -1), 0, 1)
    for t in range(SLAB):
        yf = jnp.dot(e2_ref[...], jnp.concatenate([ft[t], ft[SLAB + t]], axis=0), preferred_element_type=F32)
        yb = jnp.dot(e2_ref[...], jnp.concatenate([bt[t], bt[SLAB + t]], axis=0), preferred_element_type=F32)
        h_ref[0, t] = jnp.concatenate([yf[:nb] + yb[:nb], yf[nb:] - yb[nb:]], axis=0).astype(BF16)


def _filter_spectrum(y1, e2, *, na, nb):
    c = HYENA_WIDTH
    y5 = y1.reshape(2 * HYENA_ORDER, nb, 2, na // 2, c)
    spec = lambda d: pl.BlockSpec((1, nb, 2, SLAB, c), lambda o, k: (2 * o + d, 0, 0, k, 0))
    return pl.pallas_call(
        functools.partial(_s2f_kernel, nb=nb),
        grid=(HYENA_ORDER, na // 2 // SLAB),
        in_specs=[spec(0), spec(1), pl.BlockSpec((2 * nb, 2 * nb), lambda o, k: (0, 0))],
        out_specs=pl.BlockSpec((1, SLAB, 2 * nb, c), lambda o, k: (o, k, 0, 0)),
        out_shape=jax.ShapeDtypeStruct((HYENA_ORDER, na // 2, 2 * nb, c), BF16),
        compiler_params=_cparams("parallel", "arbitrary"),
        name=f"filter_spec_n{na * nb}",
    )(y5, y5, e2)


def _s2_kernel(y_ref, h_ref, e2_ref, e2i_ref, out_ref, ot_ref, *, nb):
    c = HYENA_WIDTH
    yt = jnp.swapaxes(y_ref[0].reshape(nb, 2 * SLAB, -1), 0, 1)
    chunk = SLAB if nb <= 64 else SLAB // 2
    for t0 in range(0, SLAB, chunk):
        ts = range(t0, t0 + chunk)
        d = jnp.concatenate([jnp.concatenate([yt[part * SLAB + t] for t in ts], axis=1)
                             for part in range(2)], axis=0)
        y = jnp.dot(e2_ref[...], d, preferred_element_type=F32)
        h = jnp.concatenate([h_ref[0, t] for t in ts], axis=1).astype(F32)
        yr, yi, hr, hi = y[:nb], y[nb:], h[:nb], h[nb:]
        p = jnp.concatenate([yr * hr - yi * hi, yr * hi + yi * hr], axis=0).astype(BF16)
        o = jnp.dot(e2i_ref[...], p, preferred_element_type=F32).astype(BF16)
        for k, t in enumerate(ts):
            ot_ref[t] = o[:nb, k * c:(k + 1) * c]
            ot_ref[SLAB + t] = o[nb:, k * c:(k + 1) * c]
    out_ref[0] = jnp.swapaxes(ot_ref[...], 0, 1).reshape(nb, 2, SLAB, -1)


def _stage2(y1, hspec, order, e2, e2i, *, na, nb):
    bx = y1.shape[0]
    c = HYENA_WIDTH
    y5 = y1.reshape(bx, nb, 2, na // 2, c)
    yspec = pl.BlockSpec((1, nb, 2, SLAB, c), lambda k, bi: (bi, 0, 0, k, 0))
    mat = pl.BlockSpec((2 * nb, 2 * nb), lambda k, bi: (0, 0))
    out = pl.pallas_call(
        functools.partial(_s2_kernel, nb=nb),
        grid=(na // 2 // SLAB, bx),
        in_specs=[yspec, pl.BlockSpec((1, SLAB, 2 * nb, c), lambda k, bi: (order, k, 0, 0)), mat, mat],
        out_specs=yspec,
        out_shape=jax.ShapeDtypeStruct((bx, nb, 2, na // 2, c), BF16),
        scratch_shapes=[pltpu.VMEM((2 * SLAB, nb, c), BF16)],
        compiler_params=_cparams("parallel", "arbitrary"),
        name=f"dft2_n{na * nb}",
    )(y5, hspec, e2, e2i)
    return out.reshape(bx, nb, na, c)


def _s1inv_chain_kernel(y_ref, g_ref, z_ref, gate_ref, d_ref, m_ref, znew_ref, y1_ref):
    zt = jnp.swapaxes(z_ref[0], 0, 1)
    gt = jnp.swapaxes(gate_ref[0], 0, 1)
    for j in range(SLAB):
        conv = jnp.dot(g_ref[j], y_ref[0, j], preferred_element_type=F32)
        zn = (gt[j].astype(F32) * (conv + d_ref[...] * zt[j].astype(F32))).astype(BF16)
        znew_ref[0, j] = zn
        y1_ref[0, j] = jnp.dot(m_ref[j], zn, preferred_element_type=F32).astype(BF16)


def _s1inv_last_kernel(y_ref, g_ref, z_ref, gate_ref, d_ref, znew_ref, zt_ref):
    gt = jnp.swapaxes(gate_ref[0], 0, 1)
    for j in range(SLAB):
        conv = jnp.dot(g_ref[j], y_ref[0, j], preferred_element_type=F32)
        zt_ref[j] = (gt[j].astype(F32) * (conv + d_ref[...] * z_ref[0, j].astype(F32))).astype(BF16)
    znew_ref[0] = jnp.swapaxes(zt_ref[...], 0, 1)


def _stage1_inverse(y3, ginv, z, gate, d, mats, *, na, nb):
    bx = y3.shape[0]
    c = HYENA_WIDTH
    half = na // 2
    l = half * nb
    nat = pl.BlockSpec((1, half, SLAB, c), lambda j, bi: (bi, 0, j, 0))
    tr = lambda rows: pl.BlockSpec((1, SLAB, rows, c), lambda j, bi: (bi, j, 0, 0))
    gspec = pl.BlockSpec((SLAB, half, na), lambda j, bi: (j, 0, 0))
    dspec = pl.BlockSpec((1, c), lambda j, bi: (0, 0))
    gate4 = gate.reshape(bx, half, nb, c)
    if mats is not None:
        return pl.pallas_call(
            _s1inv_chain_kernel,
            grid=(nb // SLAB, bx),
            in_specs=[tr(na), gspec, nat, nat, dspec, pl.BlockSpec((SLAB, na, half), lambda j, bi: (j, 0, 0))],
            out_specs=[tr(half), tr(na)],
            out_shape=[jax.ShapeDtypeStruct((bx, nb, half, c), BF16),
                       jax.ShapeDtypeStruct((bx, nb, na, c), BF16)],
            compiler_params=_cparams("parallel", "arbitrary"),
            name=f"idft1_chain_l{l}",
        )(y3, ginv, z.reshape(bx, half, nb, c), gate4, d, mats)
    out = pl.pallas_call(
        _s1inv_last_kernel,
        grid=(nb // SLAB, bx),
        in_specs=[tr(na), gspec, tr(half), nat, dspec],
        out_specs=nat,
        out_shape=jax.ShapeDtypeStruct((bx, half, nb, c), BF16),
        scratch_shapes=[pltpu.VMEM((SLAB, half, c), BF16)],
        compiler_params=_cparams("parallel", "arbitrary"),
        name=f"idft1_last_l{l}",
    )(y3, ginv, z, gate4, d)
    return out.reshape(bx, l, c)


def _hyena(z0, x1, x2, filt, hyena_d):
    bx, l, c = z0.shape
    na, nb = _fft_plan(l)
    m_fwd, ginv, e2, e2i = (jnp.asarray(t).astype(BF16) for t in _dft_tables(l))
    hspec = _filter_spectrum(_stage1(filt, m_fwd, na=na, nb=nb), e2, na=na, nb=nb)
    y1 = _stage1(z0, m_fwd, na=na, nb=nb)
    y3 = _stage2(y1, hspec, 0, e2, e2i, na=na, nb=nb)
    z1, y1 = _stage1_inverse(y3, ginv, z0, x1, hyena_d[0:1], m_fwd, na=na, nb=nb)
    y3 = _stage2(y1, hspec, 1, e2, e2i, na=na, nb=nb)
    return _stage1_inverse(y3, ginv, z1, x2, hyena_d[1:2], None, na=na, nb=nb)


def _final_kernel(attn_ref, ga_ref, z_ref, gh_ref, x_ref, ag_ref, hg_ref, wo_ref, lg_ref, lb_ref, y_ref):
    def rms(v, g):
        return v * lax.rsqrt(jnp.mean(v * v, axis=-1, keepdims=True) + RMS_EPS) * g

    ma = (rms(attn_ref[0].astype(F32), ag_ref[...]) * ga_ref[0].astype(F32)).astype(BF16)
    mh = (rms(z_ref[0].astype(F32), hg_ref[...]) * gh_ref[0].astype(F32)).astype(BF16)
    a = ATTN_WIDTH
    out = (jnp.dot(ma, wo_ref[0:a, :], preferred_element_type=F32)
           + jnp.dot(mh, wo_ref[a:, :], preferred_element_type=F32))
    h = ((2.0 * DEPTH) ** 0.25) * x_ref[0] + out
    mu = jnp.mean(h, axis=-1, keepdims=True)
    hc = h - mu
    var = jnp.mean(hc * hc, axis=-1, keepdims=True)
    y_ref[0] = hc * lax.rsqrt(var + LN_EPS) * lg_ref[...] + lb_ref[...]


def _final(attn, qkvg, z, gh, x, attn_g, hyena_g, w_out_bf, ln_g, ln_b, *, tt):
    b, l, d = x.shape
    a = ATTN_WIDTH
    tok = lambda w, col=0: pl.BlockSpec((1, tt, w), lambda bi, i: (bi, i, col))
    vec = lambda w: pl.BlockSpec((1, w), lambda bi, i: (0, 0))
    return pl.pallas_call(
        _final_kernel,
        grid=(b, l // tt),
        in_specs=[tok(a), tok(a, 3), tok(a), tok(a), tok(d),
                  vec(a), vec(a), pl.BlockSpec((d, d), lambda bi, i: (0, 0)), vec(d), vec(d)],
        out_specs=tok(d),
        out_shape=jax.ShapeDtypeStruct((b, l, d), F32),
        compiler_params=_cparams("parallel", "arbitrary"),
        name="outproj_ln",
    )(attn, qkvg, z, gh, x, attn_g[None, :], hyena_g[None, :], w_out_bf, ln_g[None, :], ln_b[None, :])


def _layer(x, w_in_bf, conv_w, conv_b, filt_w1, filt_b1, filt_w2, filt_b2, filt_w3, filt_b3,
           filt_freq, filt_w4, hyena_d, attn_norm_g, hyena_norm_g, w_out_bf, ln_g, ln_b):
    b, l, _ = x.shape
    tm = min(512, l)
    qkvg, z0, x1, x2, gh = _inproj(x, w_in_bf, conv_w, conv_b[None, :], tm=tm)
    attn = _attention(qkvg)
    filt = _filters(l, filt_w1, filt_b1, filt_w2, filt_b2, filt_w3, filt_b3, filt_freq, filt_w4)
    z = _hyena(z0, x1, x2, filt, hyena_d)
    return _final(attn, qkvg, z, gh, x, attn_norm_g, hyena_norm_g, w_out_bf, ln_g, ln_b, tt=tm)


def kernel(x_prompt, x_sample, w_in, conv_w, conv_b, filt_w1, filt_b1, filt_w2, filt_b2, filt_w3,
           filt_b3, filt_freq, filt_w4, hyena_d, attn_norm_g, hyena_norm_g, w_out, ln_g, ln_b):
    def trunk(x):
        for i in range(DEPTH):
            x = _layer(x, w_in[i].astype(BF16), conv_w[i], conv_b[i], filt_w1[i], filt_b1[i], filt_w2[i],
                       filt_b2[i], filt_w3[i], filt_b3[i], filt_freq[i], filt_w4[i], hyena_d[i],
                       attn_norm_g[i], hyena_norm_g[i], w_out[i].astype(BF16), ln_g[i], ln_b[i])
        return x
    return (trunk(x_prompt), trunk(x_sample))
```

```python
import functools
import math

import numpy as np
import jax
import jax.numpy as jnp
from jax import lax
from jax.experimental import pallas as pl
from jax.experimental.pallas import tpu as pltpu

F32 = jnp.float32
BF16 = jnp.bfloat16

D_MODEL = 1024
ATTN_WIDTH = 512
HYENA_WIDTH = 512
HEAD_DIM = 64
N_HEADS = 8
HALF_WINDOW = 64
HYENA_ORDER = 2
FILTER_BANDS = 16
FILTER_HIDDEN = 64
FAST_DECAY_PCT = 0.3
SLOW_DECAY_PCT = 1.5
DECAY_TARGET = 1e-2
PROJ_WIDTH = 4096
LN_EPS = 1e-5
RMS_EPS = 1e-6
NEG_INF = -1e30
DEPTH = 1

Q_BLOCK = 128
KEY_WINDOW = 256
SLAB = 16
HEAD_GROUP = 4
GROUP_WIDTH = HEAD_GROUP * HEAD_DIM
COLUMN_CHUNK = 256
LOG2_E = math.log2(math.e)
Q_SCALE = HEAD_DIM ** -0.5 * LOG2_E
LOOP_UNROLL = 16
VMEM_LIMIT = 56 * 1024 * 1024
STEP_INPUT_BYTES = 2 * 1024 * 1024


def _cparams(*sem):
    return pltpu.CompilerParams(dimension_semantics=sem, vmem_limit_bytes=VMEM_LIMIT)


def _silu(g):
    return g / (1.0 + jnp.exp(-g))


def _inproj_kernel(xp_ref, x_ref, xn_ref, w_ref, cw_ref, cb_ref,
                   qkvg_ref, z0_ref, x1_ref, x2_ref, gh_ref, *, tm, n_tiles):
    i = pl.program_id(1)
    a, c = ATTN_WIDTH, HYENA_WIDTH
    xb = x_ref[0].astype(BF16)

    def proj(lo, hi, lhs=xb):
        return jnp.dot(lhs, w_ref[:, lo:hi], preferred_element_type=F32)

    halo = jnp.concatenate([xp_ref[0, 0], xn_ref[0, 0]], axis=0).astype(BF16)
    xext = jnp.concatenate([xb, halo], axis=0)
    cw2 = COLUMN_CHUNK
    row = lax.broadcasted_iota(jnp.int32, (tm, cw2), 0)
    has_prev, has_next = (i > 0).astype(F32), (i < n_tiles - 1).astype(F32)
    for part, dst in enumerate((z0_ref, x1_ref, x2_ref)):
        for c0 in range(0, c, cw2):
            lo = 4 * a + part * c + c0
            pext = proj(lo, lo + cw2, xext)
            p = pext[:tm]
            below = jnp.where(row == 0, pext[tm + 7:tm + 8] * has_prev, pltpu.roll(p, 1, 0))
            above = jnp.where(row == tm - 1, pext[tm + 8:tm + 9] * has_next, pltpu.roll(p, tm - 1, 0))
            cs = slice(part * c + c0, part * c + c0 + cw2)
            dst[0, :, c0:c0 + cw2] = (cb_ref[:, cs] + below * cw_ref[0:1, cs] + p * cw_ref[1:2, cs]
                                      + above * cw_ref[2:3, cs]).astype(BF16)

    for c0 in range(0, a, cw2):
        qkvg_ref[0, :, 3 * a + c0:3 * a + c0 + cw2] = _silu(proj(3 * a + c0, 3 * a + c0 + cw2)).astype(BF16)
        gh_ref[0, :, c0:c0 + cw2] = _silu(proj(4 * a + 3 * c + c0, 4 * a + 3 * c + c0 + cw2)).astype(BF16)
        qkvg_ref[0, :, c0:c0 + cw2] = (proj(c0, c0 + cw2) * Q_SCALE).astype(BF16)
    for c0 in range(a, 3 * a, cw2):
        qkvg_ref[0, :, c0:c0 + cw2] = proj(c0, c0 + cw2).astype(BF16)


def _inproj(x, w_bf, conv_w, conv_b, *, tm):
    b, l, d = x.shape
    n_tiles = l // tm
    x8 = x.reshape(b, l // 8, 8, d)
    r = tm // 8
    n8 = l // 8
    bf_out = lambda w: jax.ShapeDtypeStruct((b, l, w), BF16)
    row_spec = lambda w: pl.BlockSpec((1, tm, w), lambda bi, i: (bi, i, 0))
    return pl.pallas_call(
        functools.partial(_inproj_kernel, tm=tm, n_tiles=n_tiles),
        grid=(b, n_tiles),
        in_specs=[
            pl.BlockSpec((1, 1, 8, d), lambda bi, i: (bi, jnp.maximum(i * r - 1, 0), 0, 0)),
            pl.BlockSpec((1, tm, d), lambda bi, i: (bi, i, 0)),
            pl.BlockSpec((1, 1, 8, d), lambda bi, i: (bi, jnp.minimum((i + 1) * r, n8 - 1), 0, 0)),
            pl.BlockSpec((d, PROJ_WIDTH), lambda bi, i: (0, 0)),
            pl.BlockSpec((3, 3 * HYENA_WIDTH), lambda bi, i: (0, 0)),
            pl.BlockSpec((1, 3 * HYENA_WIDTH), lambda bi, i: (0, 0)),
        ],
        out_specs=[row_spec(4 * ATTN_WIDTH), row_spec(HYENA_WIDTH), row_spec(HYENA_WIDTH),
                   row_spec(HYENA_WIDTH), row_spec(HYENA_WIDTH)],
        out_shape=[bf_out(4 * ATTN_WIDTH), bf_out(HYENA_WIDTH), bf_out(HYENA_WIDTH),
                   bf_out(HYENA_WIDTH), bf_out(HYENA_WIDTH)],
        compiler_params=_cparams("parallel", "arbitrary"),
        name="inproj",
    )(x8, x, x8, w_bf, conv_w, conv_b)


def _band_bias_tables():
    qi = np.arange(Q_BLOCK)[:, None]
    kj = np.arange(KEY_WINDOW)[None, :]
    rel_band = kj - HALF_WINDOW - qi
    rel_d4 = 4 * ((kj % 64) - 16 - (qi % 32)) + (kj // 64 - qi // 32)
    slopes = np.asarray([2.0 ** (-8.0 * (i + 1) / N_HEADS) for i in range(N_HEADS)], np.float32)
    out = []
    for rel, dil in ((rel_band, 1), (rel_d4, 4), (rel_band, 16)):
        rel = np.abs(rel)
        bias = -(LOG2_E * slopes[:, None, None]) * (rel * dil).astype(np.float32)[None]
        out.append(np.where(rel[None] <= HALF_WINDOW, bias, NEG_INF).astype(np.float32))
    return np.stack(out)


def _attn_kernel(q_ref, kp_ref, k_ref, kn_ref, vp_ref, v_ref, vn_ref, bias_ref, o_ref,
                 qt_ref, kt_ref, vt_ref, kn3_ref, vn3_ref, acc_ref, m_ref, l_ref, on_ref, ln_ref,
                 *, ti, s16, seq_len):
    i0 = pl.program_id(1) * ti
    hw = HALF_WINDOW
    dn_t = (((1,), (1,)), ((), ()))

    qt_ref[...] = jnp.swapaxes(q_ref[0], 0, 1)
    kt_ref[...] = jnp.swapaxes(jnp.concatenate([kp_ref[0], k_ref[0], kn_ref[0]], axis=0), 0, 1)
    vt_ref[...] = jnp.swapaxes(jnp.concatenate([vp_ref[0], v_ref[0], vn_ref[0]], axis=0), 0, 1)
    kn3_ref[...] = jnp.concatenate([kp_ref[0, hw - 4:hw], k_ref[0], kn_ref[0, 0:4]], axis=0)
    vn3_ref[...] = jnp.concatenate([vp_ref[0, hw - 4:hw], v_ref[0], vn_ref[0, 0:4]], axis=0)

    col = lax.broadcasted_iota(jnp.int32, (1, KEY_WINDOW), 1)

    pair = 2 * HEAD_DIM
    first = lax.broadcasted_iota(jnp.int32, (1, pair), 1) < HEAD_DIM
    ones_cols = jnp.ones((KEY_WINDOW, pair), BF16)

    def tile(q, kw, vw, pat, kvalid):
        ms, ls, accs = [], [], []
        for pr in range(HEAD_GROUP // 2):
            ps = slice(pr * pair, (pr + 1) * pair)
            qp, kp, vext = q[:, ps], kw[:, ps], jnp.concatenate([vw[:, ps], ones_cols], axis=1)
            halves = []
            for hh in range(2):
                own = first if hh == 0 else jnp.logical_not(first)
                s = lax.dot_general(jnp.where(own, qp, jnp.zeros_like(qp)), kp, dn_t,
                                    preferred_element_type=F32)
                s = jnp.where(kvalid, s + bias_ref[pat, 2 * pr + hh], NEG_INF)
                m = jnp.max(s, axis=-1, keepdims=True)
                pv = jnp.dot(jnp.exp2(s - m).astype(BF16), vext, preferred_element_type=F32)
                halves.append((m, pv))
            (m0, pv0), (m1, pv1) = halves
            ms.append(jnp.where(first, m0, m1))
            accs.append(jnp.where(first, pv0[:, :pair], pv1[:, :pair]))
            ls.append(jnp.where(first, pv0[:, pair:], pv1[:, pair:]))
        cat = lambda parts: jnp.concatenate(parts, axis=1)
        return cat(ms), cat(ls), cat(accs)

    i16 = i0 - hw + col

    def d16_body(t, carry):
        r = t // (ti // Q_BLOCK)
        js = pl.multiple_of((t % (ti // Q_BLOCK)) * Q_BLOCK, Q_BLOCK)
        m, l, acc = tile(qt_ref[r, pl.ds(js, Q_BLOCK), :], kt_ref[r, pl.ds(js, KEY_WINDOW), :],
                         vt_ref[r, pl.ds(js, KEY_WINDOW), :], 2, (i16 + js >= 0) & (i16 + js < s16))
        m_ref[r, pl.ds(js, Q_BLOCK), :] = m
        l_ref[r, pl.ds(js, Q_BLOCK), :] = l
        acc_ref[r, pl.ds(js, Q_BLOCK), :] = acc
        return carry

    lax.fori_loop(0, SLAB * (ti // Q_BLOCK), d16_body, 0, unroll=LOOP_UNROLL)

    n_sub = ti // 32

    def d4_body(t, carry):
        r4 = t // n_sub
        is_ = pl.multiple_of((t % n_sub) * 32, 32)
        ks = pl.multiple_of(is_ + hw - 16, 16)
        q = jnp.concatenate([qt_ref[r4 + 4 * q4, pl.ds(is_, 32), :] for q4 in range(4)], axis=0)
        kw = jnp.concatenate([kt_ref[r4 + 4 * q4, pl.ds(ks, 64), :] for q4 in range(4)], axis=0)
        vw = jnp.concatenate([vt_ref[r4 + 4 * q4, pl.ds(ks, 64), :] for q4 in range(4)], axis=0)
        i4 = i0 + is_ - 16 + (col % 64)
        m4, l4, a4 = tile(q, kw, vw, 1, (i4 >= 0) & (i4 < s16))
        for q4 in range(4):
            rows = slice(q4 * 32, (q4 + 1) * 32)
            at = (r4 + 4 * q4, pl.ds(is_, 32), slice(None))
            m_old, l_old, a_old = m_ref[at], l_ref[at], acc_ref[at]
            m_new = jnp.maximum(m_old, m4[rows])
            alpha, beta = jnp.exp2(m_old - m_new), jnp.exp2(m4[rows] - m_new)
            m_ref[at] = m_new
            l_ref[at] = l_old * alpha + l4[rows] * beta
            acc_ref[at] = a_old * alpha + a4[rows] * beta
        return carry

    lax.fori_loop(0, 4 * n_sub, d4_body, 0, unroll=LOOP_UNROLL)

    on_ref[...] = jnp.swapaxes((acc_ref[...] / l_ref[...]).astype(BF16), 0, 1)
    ln_ref[...] = jnp.swapaxes(m_ref[...] + jnp.log2(l_ref[...]), 0, 1)

    def d1_body(j, carry):
        s8 = pl.multiple_of(j * 8, 8)
        q = q_ref[0, pl.ds(s8, 8)].reshape(Q_BLOCK, GROUP_WIDTH)
        kw = kn3_ref[pl.ds(s8, SLAB)].reshape(KEY_WINDOW, GROUP_WIDTH)
        vw = vn3_ref[pl.ds(s8, SLAB)].reshape(KEY_WINDOW, GROUP_WIDTH)
        tok = SLAB * (i0 + s8 - 4) + col
        m1, l1, a1 = tile(q, kw, vw, 0, (tok >= 0) & (tok < seq_len))
        lse_p = ln_ref[pl.ds(s8, 8)].reshape(Q_BLOCK, GROUP_WIDTH)
        o_p = on_ref[pl.ds(s8, 8)].reshape(Q_BLOCK, GROUP_WIDTH).astype(F32)
        m = jnp.maximum(m1, lse_p)
        alpha, beta = jnp.exp2(m1 - m), jnp.exp2(lse_p - m)
        out = (a1 * alpha + o_p * beta) / (l1 * alpha + beta)
        o_ref[0, pl.ds(s8, 8)] = out.astype(BF16).reshape(8, SLAB, GROUP_WIDTH)
        return carry

    lax.fori_loop(0, ti // 8, d1_body, 0, unroll=LOOP_UNROLL)


def _attention(qkvg):
    b, l, w4 = qkvg.shape
    s16 = l // SLAB
    ti = min(128, s16)
    hw = HALF_WINDOW
    gw = GROUP_WIDTH
    view = qkvg.reshape(b, s16, SLAB, w4)
    ng = ATTN_WIDTH // gw
    n_halo = s16 // hw
    rq = ti // hw

    def main(part):
        return pl.BlockSpec((1, ti, SLAB, gw), lambda bi, i, g: (bi, i, 0, part * ng + g))

    def prev(part):
        return pl.BlockSpec((1, hw, SLAB, gw),
                            lambda bi, i, g: (bi, jnp.maximum(i * rq - 1, 0), 0, part * ng + g))

    def nxt(part):
        return pl.BlockSpec((1, hw, SLAB, gw),
                            lambda bi, i, g: (bi, jnp.minimum((i + 1) * rq, n_halo - 1), 0, part * ng + g))

    vm = lambda shape, dt: pltpu.VMEM(shape, dt)
    out = pl.pallas_call(
        functools.partial(_attn_kernel, ti=ti, s16=s16, seq_len=l),
        grid=(b, s16 // ti, ng),
        in_specs=[main(0), prev(1), main(1), nxt(1), prev(2), main(2), nxt(2),
                  pl.BlockSpec((3, HEAD_GROUP, Q_BLOCK, KEY_WINDOW), lambda bi, i, g: (0, g, 0, 0))],
        out_specs=pl.BlockSpec((1, ti, SLAB, gw), lambda bi, i, g: (bi, i, 0, g)),
        out_shape=jax.ShapeDtypeStruct((b, s16, SLAB, ATTN_WIDTH), BF16),
        scratch_shapes=[vm((SLAB, ti, gw), BF16), vm((SLAB, ti + 2 * hw, gw), BF16),
                        vm((SLAB, ti + 2 * hw, gw), BF16),
                        vm((ti + 8, SLAB, gw), BF16), vm((ti + 8, SLAB, gw), BF16),
                        vm((SLAB, ti, gw), F32), vm((SLAB, ti, gw), F32), vm((SLAB, ti, gw), F32),
                        vm((ti, SLAB, gw), BF16), vm((ti, SLAB, gw), F32)],
        compiler_params=_cparams("parallel", "parallel", "arbitrary"),
        name="band_attn",
    )(view, view, view, view, view, view, view, jnp.asarray(_band_bias_tables()))
    return out.reshape(b, l, ATTN_WIDTH)


def _filter_kernel(w1t_ref, b1_ref, w2t_ref, b2_ref, w3t_ref, b3_ref, fq_ref, w4_ref, fr_ref, dl_ref,
                   out_ref, *, tn, l):
    hp = lax.Precision.HIGHEST
    c = HYENA_WIDTH
    nbd = FILTER_BANDS
    pos = (pl.program_id(0) * tn + lax.broadcasted_iota(jnp.int32, (1, tn), 1)).astype(F32)
    t = pos * (1.0 / (l - 1))
    ang = fr_ref[...] * ((2.0 * math.pi / l) * pos)
    dot = lambda a, b: jnp.dot(a, b, precision=hp, preferred_element_type=F32)
    h = (w1t_ref[:, 0:1] * t + dot(w1t_ref[:, 1:1 + nbd], jnp.cos(ang))
         - dot(w1t_ref[:, 1 + nbd:1 + 2 * nbd], jnp.sin(ang)))
    h = jnp.sin(fq_ref[:, 0:1] * (h + b1_ref[...]))
    h = jnp.sin(fq_ref[:, 1:2] * (dot(w2t_ref[...], h) + b2_ref[...]))
    h = jnp.sin(fq_ref[:, 2:3] * (dot(w3t_ref[...], h) + b3_ref[...]))
    ht = h.T.astype(BF16)
    n_col = pl.program_id(0) * tn + lax.broadcasted_iota(jnp.int32, (tn, 1), 0)
    decay = jnp.exp(-(n_col.astype(F32) * (1.0 / (l - 1))) * dl_ref[...])
    decay_b = jnp.where(n_col == 0, 0.0, decay)
    for od in range(2 * HYENA_ORDER):
        f = jnp.dot(ht, w4_ref[:, od * c:(od + 1) * c].astype(BF16), preferred_element_type=F32)
        out_ref[od] = (f * (decay_b if od % 2 else decay)).astype(BF16)


def _filters(l, w1, b1, w2, b2, w3, b3, freq, w4):
    tn = min(512, l)
    bands = FILTER_BANDS
    fr = jnp.asarray(np.linspace(1e-4, bands - 1, bands, dtype=np.float32)[:, None])
    max_decay = math.log(DECAY_TARGET) / FAST_DECAY_PCT
    min_decay = math.log(DECAY_TARGET) / SLOW_DECAY_PCT
    deltas = jnp.asarray(np.abs(np.linspace(min_decay, max_decay, HYENA_WIDTH, dtype=np.float32))[None, :])
    full = lambda arr: pl.BlockSpec(arr.shape, lambda i: (0,) * arr.ndim)
    args = (w1.T, b1[:, None], w2.T, b2[:, None], w3.T, b3[:, None], freq.T, w4, fr, deltas)
    return pl.pallas_call(
        functools.partial(_filter_kernel, tn=tn, l=l),
        grid=(l // tn,),
        in_specs=[full(a) for a in args],
        out_specs=pl.BlockSpec((2 * HYENA_ORDER, tn, HYENA_WIDTH), lambda i: (0, i, 0)),
        out_shape=jax.ShapeDtypeStruct((2 * HYENA_ORDER, l, HYENA_WIDTH), BF16),
        compiler_params=_cparams("parallel"),
        name=f"filters_l{l}",
    )(*args)


def _fft_plan(l):
    n = 2 * l
    na = 256 if n >= 32768 else (128 if n >= 8192 else 64)
    return na, n // na


@functools.lru_cache(maxsize=None)
def _dft_tables(l):
    na, nb = _fft_plan(l)
    n = na * nb
    ka = np.arange(na // 2, dtype=np.int64)
    nas = np.arange(na // 2, dtype=np.int64)
    nbs = np.arange(nb, dtype=np.int64)
    num = ((2 * ka[None, :, None] + 1) * (nb * nas[None, None, :] + nbs[:, None, None])) % (2 * n)
    theta = num.astype(np.float64) * (math.pi / n)
    fwd = np.concatenate([np.cos(theta), -np.sin(theta)], axis=1)
    thetat = np.swapaxes(theta, 1, 2)
    inv = (2.0 / n) * np.concatenate([np.cos(thetat), -np.sin(thetat)], axis=2)
    phi = ((nbs[:, None] * nbs[None, :]) % nb).astype(np.float64) * (2.0 * math.pi / nb)
    cc, ss = np.cos(phi), np.sin(phi)
    e2 = np.block([[cc, ss], [-ss, cc]])
    e2i = np.block([[cc, -ss], [ss, cc]])
    e2fb = np.concatenate([e2, np.block([[cc, ss], [ss, -cc]])], axis=1)
    cast = lambda arr: np.ascontiguousarray(arr.astype(np.float32))
    return cast(fwd), cast(inv), cast(e2), cast(e2i), cast(e2fb)


def _rows_per_step(half, nb):
    want = max(1, STEP_INPUT_BYTES // (half * SLAB * HYENA_WIDTH * 2))
    return min(nb, SLAB * want)


def _s1_kernel(src_ref, m_ref, out_ref):
    xt = jnp.swapaxes(src_ref[0], 0, 1)
    for j in range(xt.shape[0]):
        out_ref[0, j] = jnp.dot(m_ref[j], xt[j], preferred_element_type=F32).astype(BF16)


def _stage1(src, mats, *, na, nb):
    bx, l, c = src.shape
    half = na // 2
    rj = _rows_per_step(half, nb)
    return pl.pallas_call(
        _s1_kernel,
        grid=(nb // rj, bx),
        in_specs=[pl.BlockSpec((1, half, rj, c), lambda j, bi: (bi, 0, j, 0)),
                  pl.BlockSpec((rj, na, half), lambda j, bi: (j, 0, 0))],
        out_specs=pl.BlockSpec((1, rj, na, c), lambda j, bi: (bi, j, 0, 0)),
        out_shape=jax.ShapeDtypeStruct((bx, nb, na, c), BF16),
        compiler_params=_cparams("parallel", "arbitrary"),
        name=f"dft1_l{l}",
    )(src.reshape(bx, half, nb, c), mats)


def _s2f_kernel(yf_ref, yb_ref, e2_ref, h_ref, *, nb):
    c = HYENA_WIDTH
    ft = jnp.swapaxes(yf_ref[0].reshape(nb, 2 * SLAB, -1), 0, 1)
    bt = jnp.swapaxes(yb_ref[0].reshape(nb, 2 * SLAB, -1), 0, 1)
    chunk = SLAB if nb <= 64 else SLAB // 2
    for t0 in range(0, SLAB, chunk):
        ts = range(t0, t0 + chunk)
        rows = [jnp.concatenate([src[part * SLAB + t] for t in ts], axis=1)
                for src in (ft, bt) for part in range(2)]
        y = jnp.dot(e2_ref[...], jnp.concatenate(rows, axis=0), preferred_element_type=F32).astype(BF16)
        for k, t in enumerate(ts):
            h_ref[0, t] = y[:, k * c:(k + 1) * c]


def _filter_spectrum(y1, e2, *, na, nb):
    c = HYENA_WIDTH
    y5 = y1.reshape(2 * HYENA_ORDER, nb, 2, na // 2, c)
    spec = lambda d: pl.BlockSpec((1, nb, 2, SLAB, c), lambda o, k: (2 * o + d, 0, 0, k, 0))
    return pl.pallas_call(
        functools.partial(_s2f_kernel, nb=nb),
        grid=(HYENA_ORDER, na // 2 // SLAB),
        in_specs=[spec(0), spec(1), pl.BlockSpec((2 * nb, 4 * nb), lambda o, k: (0, 0))],
        out_specs=pl.BlockSpec((1, SLAB, 2 * nb, c), lambda o, k: (o, k, 0, 0)),
        out_shape=jax.ShapeDtypeStruct((HYENA_ORDER, na // 2, 2 * nb, c), BF16),
        compiler_params=_cparams("parallel", "arbitrary"),
        name=f"filter_spec_n{na * nb}",
    )(y5, y5, e2)


def _s2_kernel(y_ref, h_ref, e2_ref, e2i_ref, out_ref, ot_ref, *, nb):
    c = HYENA_WIDTH
    yt = jnp.swapaxes(y_ref[0].reshape(nb, 2 * SLAB, -1), 0, 1)
    chunk = SLAB if nb <= 64 else SLAB // 2
    for t0 in range(0, SLAB, chunk):
        ts = range(t0, t0 + chunk)
        d = jnp.concatenate([jnp.concatenate([yt[part * SLAB + t] for t in ts], axis=1)
                             for part in range(2)], axis=0)
        y = jnp.dot(e2_ref[...], d, preferred_element_type=F32)
        h = jnp.concatenate([h_ref[0, t] for t in ts], axis=1).astype(F32)
        yr, yi, hr, hi = y[:nb], y[nb:], h[:nb], h[nb:]
        p = jnp.concatenate([yr * hr - yi * hi, yr * hi + yi * hr], axis=0).astype(BF16)
        o = jnp.dot(e2i_ref[...], p, preferred_element_type=F32).astype(BF16)
        for k, t in enumerate(ts):
            ot_ref[t] = o[:nb, k * c:(k + 1) * c]
            ot_ref[SLAB + t] = o[nb:, k * c:(k + 1) * c]
    out_ref[0] = jnp.swapaxes(ot_ref[...], 0, 1).reshape(nb, 2, SLAB, -1)


def _stage2(y1, hspec, order, e2, e2i, *, na, nb):
    bx = y1.shape[0]
    c = HYENA_WIDTH
    y5 = y1.reshape(bx, nb, 2, na // 2, c)
    yspec = pl.BlockSpec((1, nb, 2, SLAB, c), lambda k, bi: (bi, 0, 0, k, 0))
    mat = pl.BlockSpec((2 * nb, 2 * nb), lambda k, bi: (0, 0))
    out = pl.pallas_call(
        functools.partial(_s2_kernel, nb=nb),
        grid=(na // 2 // SLAB, bx),
        in_specs=[yspec, pl.BlockSpec((1, SLAB, 2 * nb, c), lambda k, bi: (order, k, 0, 0)), mat, mat],
        out_specs=yspec,
        out_shape=jax.ShapeDtypeStruct((bx, nb, 2, na // 2, c), BF16),
        scratch_shapes=[pltpu.VMEM((2 * SLAB, nb, c), BF16)],
        compiler_params=_cparams("parallel", "arbitrary"),
        name=f"dft2_n{na * nb}",
    )(y5, hspec, e2, e2i)
    return out.reshape(bx, nb, na, c)


def _s1inv_chain_kernel(y_ref, g_ref, z_ref, gate_ref, d_ref, m_ref, znew_ref, y1_ref):
    zt = jnp.swapaxes(z_ref[0], 0, 1)
    gt = jnp.swapaxes(gate_ref[0], 0, 1)
    for j in range(zt.shape[0]):
        conv = jnp.dot(g_ref[j], y_ref[0, j], preferred_element_type=F32)
        zn = (gt[j].astype(F32) * (conv + d_ref[...] * zt[j].astype(F32))).astype(BF16)
        znew_ref[0, j] = zn
        y1_ref[0, j] = jnp.dot(m_ref[j], zn, preferred_element_type=F32).astype(BF16)


def _s1inv_last_kernel(y_ref, g_ref, z_ref, gate_ref, d_ref, znew_ref, zt_ref):
    gt = jnp.swapaxes(gate_ref[0], 0, 1)
    for j in range(gt.shape[0]):
        conv = jnp.dot(g_ref[j], y_ref[0, j], preferred_element_type=F32)
        zt_ref[j] = (gt[j].astype(F32) * (conv + d_ref[...] * z_ref[0, j].astype(F32))).astype(BF16)
    znew_ref[0] = jnp.swapaxes(zt_ref[...], 0, 1)


def _stage1_inverse(y3, ginv, z, gate, d, mats, *, na, nb):
    bx = y3.shape[0]
    c = HYENA_WIDTH
    half = na // 2
    l = half * nb
    rj = _rows_per_step(half, nb)
    nat = pl.BlockSpec((1, half, rj, c), lambda j, bi: (bi, 0, j, 0))
    tr = lambda rows: pl.BlockSpec((1, rj, rows, c), lambda j, bi: (bi, j, 0, 0))
    gspec = pl.BlockSpec((rj, half, na), lambda j, bi: (j, 0, 0))
    dspec = pl.BlockSpec((1, c), lambda j, bi: (0, 0))
    gate4 = gate.reshape(bx, half, nb, c)
    if mats is not None:
        return pl.pallas_call(
            _s1inv_chain_kernel,
            grid=(nb // rj, bx),
            in_specs=[tr(na), gspec, nat, nat, dspec, pl.BlockSpec((rj, na, half), lambda j, bi: (j, 0, 0))],
            out_specs=[tr(half), tr(na)],
            out_shape=[jax.ShapeDtypeStruct((bx, nb, half, c), BF16),
                       jax.ShapeDtypeStruct((bx, nb, na, c), BF16)],
            compiler_params=_cparams("parallel", "arbitrary"),
            name=f"idft1_chain_l{l}",
        )(y3, ginv, z.reshape(bx, half, nb, c), gate4, d, mats)
    out = pl.pallas_call(
        _s1inv_last_kernel,
        grid=(nb // rj, bx),
        in_specs=[tr(na), gspec, tr(half), nat, dspec],
        out_specs=nat,
        out_shape=jax.ShapeDtypeStruct((bx, half, nb, c), BF16),
        scratch_shapes=[pltpu.VMEM((rj, half, c), BF16)],
        compiler_params=_cparams("parallel", "arbitrary"),
        name=f"idft1_last_l{l}",
    )(y3, ginv, z, gate4, d)
    return out.reshape(bx, l, c)


def _hyena(z0, x1, x2, filt, hyena_d):
    bx, l, c = z0.shape
    na, nb = _fft_plan(l)
    m_fwd, ginv, e2, e2i, e2fb = (jnp.asarray(t).astype(BF16) for t in _dft_tables(l))
    hspec = _filter_spectrum(_stage1(filt, m_fwd, na=na, nb=nb), e2fb, na=na, nb=nb)
    y1 = _stage1(z0, m_fwd, na=na, nb=nb)
    y3 = _stage2(y1, hspec, 0, e2, e2i, na=na, nb=nb)
    z1, y1 = _stage1_inverse(y3, ginv, z0, x1, hyena_d[0:1], m_fwd, na=na, nb=nb)
    y3 = _stage2(y1, hspec, 1, e2, e2i, na=na, nb=nb)
    return _stage1_inverse(y3, ginv, z1, x2, hyena_d[1:2], None, na=na, nb=nb)


def _final_kernel(attn_ref, ga_ref, z_ref, gh_ref, x_ref, ag_ref, hg_ref, wo_ref, lg_ref, lb_ref, y_ref):
    def rms(v, g):
        return v * lax.rsqrt(jnp.mean(v * v, axis=-1, keepdims=True) + RMS_EPS) * g

    ma = (rms(attn_ref[0].astype(F32), ag_ref[...]) * ga_ref[0].astype(F32)).astype(BF16)
    mh = (rms(z_ref[0].astype(F32), hg_ref[...]) * gh_ref[0].astype(F32)).astype(BF16)
    a = ATTN_WIDTH
    out = (jnp.dot(ma, wo_ref[0:a, :], preferred_element_type=F32)
           + jnp.dot(mh, wo_ref[a:, :], preferred_element_type=F32))
    h = ((2.0 * DEPTH) ** 0.25) * x_ref[0] + out
    mu = jnp.mean(h, axis=-1, keepdims=True)
    hc = h - mu
    var = jnp.mean(hc * hc, axis=-1, keepdims=True)
    y_ref[0] = hc * lax.rsqrt(var + LN_EPS) * lg_ref[...] + lb_ref[...]


def _final(attn, qkvg, z, gh, x, attn_g, hyena_g, w_out_bf, ln_g, ln_b, *, tt):
    b, l, d = x.shape
    a = ATTN_WIDTH
    tok = lambda w, col=0: pl.BlockSpec((1, tt, w), lambda bi, i: (bi, i, col))
    vec = lambda w: pl.BlockSpec((1, w), lambda bi, i: (0, 0))
    return pl.pallas_call(
        _final_kernel,
        grid=(b, l // tt),
        in_specs=[tok(a), tok(a, 3), tok(a), tok(a), tok(d),
                  vec(a), vec(a), pl.BlockSpec((d, d), lambda bi, i: (0, 0)), vec(d), vec(d)],
        out_specs=tok(d),
        out_shape=jax.ShapeDtypeStruct((b, l, d), F32),
        compiler_params=_cparams("parallel", "arbitrary"),
        name="outproj_ln",
    )(attn, qkvg, z, gh, x, attn_g[None, :], hyena_g[None, :], w_out_bf, ln_g[None, :], ln_b[None, :])


def _layer(x, w_in_bf, conv_w, conv_b, filt_w1, filt_b1, filt_w2, filt_b2, filt_w3, filt_b3,
           filt_freq, filt_w4, hyena_d, attn_norm_g, hyena_norm_g, w_out_bf, ln_g, ln_b):
    b, l, _ = x.shape
    tm = min(512, l)
    qkvg, z0, x1, x2, gh = _inproj(x, w_in_bf, conv_w, conv_b[None, :], tm=tm)
    attn = _attention(qkvg)
    filt = _filters(l, filt_w1, filt_b1, filt_w2, filt_b2, filt_w3, filt_b3, filt_freq, filt_w4)
    z = _hyena(z0, x1, x2, filt, hyena_d)
    return _final(attn, qkvg, z, gh, x, attn_norm_g, hyena_norm_g, w_out_bf, ln_g, ln_b, tt=tm)


def kernel(x_prompt, x_sample, w_in, conv_w, conv_b, filt_w1, filt_b1, filt_w2, filt_b2, filt_w3,
           filt_b3, filt_freq, filt_w4, hyena_d, attn_norm_g, hyena_norm_g, w_out, ln_g, ln_b):
    def trunk(x):
        for i in range(DEPTH):
            x = _layer(x, w_in[i].astype(BF16), conv_w[i], conv_b[i], filt_w1[i], filt_b1[i], filt_w2[i],
                       filt_b2[i], filt_w3[i], filt_b3[i], filt_freq[i], filt_w4[i], hyena_d[i],
                       attn_norm_g[i], hyena_norm_g[i], w_out[i].astype(BF16), ln_g[i], ln_b[i])
        return x
    return (trunk(x_prompt), trunk(x_sample))
```

```python
import functools
import math

import numpy as np
import jax
import jax.numpy as jnp
from jax import lax
from jax.experimental import pallas as pl
from jax.experimental.pallas import tpu as pltpu

F32 = jnp.float32
BF16 = jnp.bfloat16

D_MODEL = 1024
ATTN_WIDTH = 512
HYENA_WIDTH = 512
HEAD_DIM = 64
N_HEADS = 8
HALF_WINDOW = 64
HYENA_ORDER = 2
FILTER_BANDS = 16
FILTER_HIDDEN = 64
FAST_DECAY_PCT = 0.3
SLOW_DECAY_PCT = 1.5
DECAY_TARGET = 1e-2
PROJ_WIDTH = 4096
LN_EPS = 1e-5
RMS_EPS = 1e-6
NEG_INF = -1e30
DEPTH = 1

Q_BLOCK = 128
KEY_WINDOW = 256
SLAB = 16
HEAD_GROUP = 4
GROUP_WIDTH = HEAD_GROUP * HEAD_DIM
COLUMN_CHUNK = 256
LOG2_E = math.log2(math.e)
Q_SCALE = HEAD_DIM ** -0.5 * LOG2_E
VMEM_LIMIT = 56 * 1024 * 1024
FILTER_TILE = 512
STEP_INPUT_BYTES = 2 * 1024 * 1024


def _cparams(*sem):
    return pltpu.CompilerParams(dimension_semantics=sem, vmem_limit_bytes=VMEM_LIMIT)


def _silu(g):
    return g / (1.0 + jnp.exp(-g))


def _inproj_kernel(xp_ref, x_ref, xn_ref, w_ref, cw_ref, cb_ref,
                   qkvg_ref, z0_ref, x1_ref, x2_ref, gh_ref, *, tm, n_tiles):
    i = pl.program_id(1)
    a, c = ATTN_WIDTH, HYENA_WIDTH
    xb = x_ref[0].astype(BF16)

    def proj(lo, hi, lhs=xb):
        return jnp.dot(lhs, w_ref[:, lo:hi], preferred_element_type=F32)

    halo = jnp.concatenate([xp_ref[0, 0], xn_ref[0, 0]], axis=0).astype(BF16)
    xext = jnp.concatenate([xb, halo], axis=0)
    cw2 = COLUMN_CHUNK
    row = lax.broadcasted_iota(jnp.int32, (tm, cw2), 0)
    has_prev, has_next = (i > 0).astype(F32), (i < n_tiles - 1).astype(F32)
    for part, dst in enumerate((z0_ref, x1_ref, x2_ref)):
        for c0 in range(0, c, cw2):
            lo = 4 * a + part * c + c0
            pext = proj(lo, lo + cw2, xext)
            p = pext[:tm]
            below = jnp.where(row == 0, pext[tm + 7:tm + 8] * has_prev, pltpu.roll(p, 1, 0))
            above = jnp.where(row == tm - 1, pext[tm + 8:tm + 9] * has_next, pltpu.roll(p, tm - 1, 0))
            cs = slice(part * c + c0, part * c + c0 + cw2)
            dst[0, :, c0:c0 + cw2] = (cb_ref[:, cs] + below * cw_ref[0:1, cs] + p * cw_ref[1:2, cs]
                                      + above * cw_ref[2:3, cs]).astype(BF16)

    for c0 in range(0, a, cw2):
        qkvg_ref[0, :, 3 * a + c0:3 * a + c0 + cw2] = _silu(proj(3 * a + c0, 3 * a + c0 + cw2)).astype(BF16)
        gh_ref[0, :, c0:c0 + cw2] = _silu(proj(4 * a + 3 * c + c0, 4 * a + 3 * c + c0 + cw2)).astype(BF16)
        qkvg_ref[0, :, c0:c0 + cw2] = (proj(c0, c0 + cw2) * Q_SCALE).astype(BF16)
    for c0 in range(a, 3 * a, cw2):
        qkvg_ref[0, :, c0:c0 + cw2] = proj(c0, c0 + cw2).astype(BF16)


def _inproj(x, w_bf, conv_w, conv_b, *, tm):
    b, l, d = x.shape
    n_tiles = l // tm
    x8 = x.reshape(b, l // 8, 8, d)
    r = tm // 8
    n8 = l // 8
    bf_out = lambda w: jax.ShapeDtypeStruct((b, l, w), BF16)
    row_spec = lambda w: pl.BlockSpec((1, tm, w), lambda bi, i: (bi, i, 0))
    return pl.pallas_call(
        functools.partial(_inproj_kernel, tm=tm, n_tiles=n_tiles),
        grid=(b, n_tiles),
        in_specs=[
            pl.BlockSpec((1, 1, 8, d), lambda bi, i: (bi, jnp.maximum(i * r - 1, 0), 0, 0)),
            pl.BlockSpec((1, tm, d), lambda bi, i: (bi, i, 0)),
            pl.BlockSpec((1, 1, 8, d), lambda bi, i: (bi, jnp.minimum((i + 1) * r, n8 - 1), 0, 0)),
            pl.BlockSpec((d, PROJ_WIDTH), lambda bi, i: (0, 0)),
            pl.BlockSpec((3, 3 * HYENA_WIDTH), lambda bi, i: (0, 0)),
            pl.BlockSpec((1, 3 * HYENA_WIDTH), lambda bi, i: (0, 0)),
        ],
        out_specs=[row_spec(4 * ATTN_WIDTH), row_spec(HYENA_WIDTH), row_spec(HYENA_WIDTH),
                   row_spec(HYENA_WIDTH), row_spec(HYENA_WIDTH)],
        out_shape=[bf_out(4 * ATTN_WIDTH), bf_out(HYENA_WIDTH), bf_out(HYENA_WIDTH),
                   bf_out(HYENA_WIDTH), bf_out(HYENA_WIDTH)],
        compiler_params=_cparams("parallel", "arbitrary"),
        name="inproj",
    )(x8, x, x8, w_bf, conv_w, conv_b)


def _band_bias_tables():
    qi = np.arange(Q_BLOCK)[:, None]
    kj = np.arange(KEY_WINDOW)[None, :]
    rel_band = kj - HALF_WINDOW - qi
    rel_d4 = 4 * ((kj % 64) - 16 - (qi % 32)) + (kj // 64 - qi // 32)
    slopes = np.asarray([2.0 ** (-8.0 * (i + 1) / N_HEADS) for i in range(N_HEADS)], np.float32)
    out = []
    for rel, dil in ((rel_band, 1), (rel_d4, 4), (rel_band, 16)):
        rel = np.abs(rel)
        bias = -(LOG2_E * slopes[:, None, None]) * (rel * dil).astype(np.float32)[None]
        out.append(np.where(rel[None] <= HALF_WINDOW, bias, NEG_INF).astype(np.float32))
    return np.stack(out)


def _attn_kernel(q_ref, kp_ref, k_ref, kn_ref, vp_ref, v_ref, vn_ref, bias_ref, o_ref,
                 qt_ref, kt_ref, vt_ref, acc_ref, m_ref, l_ref, on_ref, ln_ref,
                 *, ti, s16, seq_len):
    i0 = pl.program_id(1) * ti
    hw = HALF_WINDOW
    dn_t = (((1,), (1,)), ((), ()))

    qt_ref[...] = jnp.swapaxes(q_ref[0], 0, 1)
    kt_ref[...] = jnp.swapaxes(jnp.concatenate([kp_ref[0], k_ref[0], kn_ref[0]], axis=0), 0, 1)
    vt_ref[...] = jnp.swapaxes(jnp.concatenate([vp_ref[0], v_ref[0], vn_ref[0]], axis=0), 0, 1)

    col = lax.broadcasted_iota(jnp.int32, (1, KEY_WINDOW), 1)

    pair = 2 * HEAD_DIM
    first = lax.broadcasted_iota(jnp.int32, (1, pair), 1) < HEAD_DIM
    ones_cols = jnp.ones((KEY_WINDOW, pair), BF16)

    def tile(q, kw, vw, pat, kvalid):
        ms, ls, accs = [], [], []
        for pr in range(HEAD_GROUP // 2):
            ps = slice(pr * pair, (pr + 1) * pair)
            qp, kp, vext = q[:, ps], kw[:, ps], jnp.concatenate([vw[:, ps], ones_cols], axis=1)
            halves = []
            for hh in range(2):
                own = first if hh == 0 else jnp.logical_not(first)
                s = lax.dot_general(jnp.where(own, qp, jnp.zeros_like(qp)), kp, dn_t,
                                    preferred_element_type=F32)
                s = s + bias_ref[pat, 2 * pr + hh]
                if kvalid is not None:
                    s = jnp.where(kvalid, s, NEG_INF)
                m = jnp.max(s, axis=-1, keepdims=True)
                pv = jnp.dot(jnp.exp2(s - m).astype(BF16), vext, preferred_element_type=F32)
                halves.append((m, pv))
            (m0, pv0), (m1, pv1) = halves
            ms.append(jnp.where(first, m0, m1))
            accs.append(jnp.where(first, pv0[:, :pair], pv1[:, :pair]))
            ls.append(jnp.where(first, pv0[:, pair:], pv1[:, pair:]))
        cat = lambda parts: jnp.concatenate(parts, axis=1)
        return cat(ms), cat(ls), cat(accs)

    for r in range(SLAB):
        for js in range(0, ti, Q_BLOCK):
            i16 = i0 + js - hw + col
            m, l, acc = tile(qt_ref[r, js:js + Q_BLOCK, :], kt_ref[r, js:js + KEY_WINDOW, :],
                             vt_ref[r, js:js + KEY_WINDOW, :], 2, (i16 >= 0) & (i16 < s16))
            m_ref[r, js:js + Q_BLOCK, :] = m
            l_ref[r, js:js + Q_BLOCK, :] = l
            acc_ref[r, js:js + Q_BLOCK, :] = acc

    for r4 in range(4):
        for is_ in range(0, ti, 32):
            ks = is_ + hw - 16
            q = jnp.concatenate([qt_ref[r4 + 4 * q4, is_:is_ + 32, :] for q4 in range(4)], axis=0)
            kw = jnp.concatenate([kt_ref[r4 + 4 * q4, ks:ks + 64, :] for q4 in range(4)], axis=0)
            vw = jnp.concatenate([vt_ref[r4 + 4 * q4, ks:ks + 64, :] for q4 in range(4)], axis=0)
            i4 = i0 + is_ - 16 + (col % 64)
            inside = is_ - 16 >= 0 and is_ + 48 <= ti
            m4, l4, a4 = tile(q, kw, vw, 1, None if inside else (i4 >= 0) & (i4 < s16))
            for q4 in range(4):
                rows = slice(q4 * 32, (q4 + 1) * 32)
                at = (r4 + 4 * q4, slice(is_, is_ + 32), slice(None))
                m_old, l_old, a_old = m_ref[at], l_ref[at], acc_ref[at]
                m_new = jnp.maximum(m_old, m4[rows])
                alpha, beta = jnp.exp2(m_old - m_new), jnp.exp2(m4[rows] - m_new)
                m_ref[at] = m_new
                l_ref[at] = l_old * alpha + l4[rows] * beta
                acc_ref[at] = a_old * alpha + a4[rows] * beta

    on_ref[...] = jnp.swapaxes((acc_ref[...] / l_ref[...]).astype(BF16), 0, 1)
    ln_ref[...] = jnp.swapaxes(m_ref[...] + jnp.log2(l_ref[...]), 0, 1)

    def window(prev_ref, ref, next_ref, s8):
        lo, hi = s8 - 4, s8 + 12
        parts = []
        if lo < 0:
            parts.append(prev_ref[0, hw + lo:hw])
        parts.append(ref[0, max(lo, 0):min(hi, ti)])
        if hi > ti:
            parts.append(next_ref[0, 0:hi - ti])
        w = parts[0] if len(parts) == 1 else jnp.concatenate(parts, axis=0)
        return w.reshape(KEY_WINDOW, GROUP_WIDTH)

    for s8 in range(0, ti, 8):
        q = q_ref[0, s8:s8 + 8].reshape(Q_BLOCK, GROUP_WIDTH)
        kw = window(kp_ref, k_ref, kn_ref, s8)
        vw = window(vp_ref, v_ref, vn_ref, s8)
        tok = SLAB * (i0 + s8 - 4) + col
        inside = s8 - 4 >= 0 and s8 + 12 <= ti
        m1, l1, a1 = tile(q, kw, vw, 0, None if inside else (tok >= 0) & (tok < seq_len))
        lse_p = ln_ref[s8:s8 + 8].reshape(Q_BLOCK, GROUP_WIDTH)
        o_p = on_ref[s8:s8 + 8].reshape(Q_BLOCK, GROUP_WIDTH).astype(F32)
        m = jnp.maximum(m1, lse_p)
        alpha, beta = jnp.exp2(m1 - m), jnp.exp2(lse_p - m)
        out = (a1 * alpha + o_p * beta) / (l1 * alpha + beta)
        o_ref[0, s8:s8 + 8] = out.astype(BF16).reshape(8, SLAB, GROUP_WIDTH)


def _attention(qkvg):
    b, l, w4 = qkvg.shape
    s16 = l // SLAB
    ti = min(128, s16)
    hw = HALF_WINDOW
    gw = GROUP_WIDTH
    view = qkvg.reshape(b, s16, SLAB, w4)
    ng = ATTN_WIDTH // gw
    n_halo = s16 // hw
    rq = ti // hw

    def main(part):
        return pl.BlockSpec((1, ti, SLAB, gw), lambda bi, i, g: (bi, i, 0, part * ng + g))

    def prev(part):
        return pl.BlockSpec((1, hw, SLAB, gw),
                            lambda bi, i, g: (bi, jnp.maximum(i * rq - 1, 0), 0, part * ng + g))

    def nxt(part):
        return pl.BlockSpec((1, hw, SLAB, gw),
                            lambda bi, i, g: (bi, jnp.minimum((i + 1) * rq, n_halo - 1), 0, part * ng + g))

    vm = lambda shape, dt: pltpu.VMEM(shape, dt)
    out = pl.pallas_call(
        functools.partial(_attn_kernel, ti=ti, s16=s16, seq_len=l),
        grid=(b, s16 // ti, ng),
        in_specs=[main(0), prev(1), main(1), nxt(1), prev(2), main(2), nxt(2),
                  pl.BlockSpec((3, HEAD_GROUP, Q_BLOCK, KEY_WINDOW), lambda bi, i, g: (0, g, 0, 0))],
        out_specs=pl.BlockSpec((1, ti, SLAB, gw), lambda bi, i, g: (bi, i, 0, g)),
        out_shape=jax.ShapeDtypeStruct((b, s16, SLAB, ATTN_WIDTH), BF16),
        scratch_shapes=[vm((SLAB, ti, gw), BF16), vm((SLAB, ti + 2 * hw, gw), BF16),
                        vm((SLAB, ti + 2 * hw, gw), BF16),
                        vm((SLAB, ti, gw), F32), vm((SLAB, ti, gw), F32), vm((SLAB, ti, gw), F32),
                        vm((ti, SLAB, gw), BF16), vm((ti, SLAB, gw), F32)],
        compiler_params=_cparams("parallel", "parallel", "arbitrary"),
        name="band_attn",
    )(view, view, view, view, view, view, view, jnp.asarray(_band_bias_tables()))
    return out.reshape(b, l, ATTN_WIDTH)


def _filter_kernel(w1t_ref, b1_ref, w2t_ref, b2_ref, w3t_ref, b3_ref, fq_ref, w4_ref, fr_ref, dl_ref,
                   m_ref, out_ref, *, rj, half, nb, l):
    hp = lax.Precision.HIGHEST
    c = HYENA_WIDTH
    nbd = FILTER_BANDS
    tn = rj * half
    j0 = pl.program_id(0) * rj

    def lags(shape, axis):
        k = lax.broadcasted_iota(jnp.int32, shape, axis)
        return nb * (k % half) + j0 + k // half

    pos = lags((1, tn), 1).astype(F32)
    t = pos * (1.0 / (l - 1))
    ang = fr_ref[...] * ((2.0 * math.pi / l) * pos)
    dot = lambda a, b: jnp.dot(a, b, precision=hp, preferred_element_type=F32)
    h = (w1t_ref[:, 0:1] * t + dot(w1t_ref[:, 1:1 + nbd], jnp.cos(ang))
         - dot(w1t_ref[:, 1 + nbd:1 + 2 * nbd], jnp.sin(ang)))
    h = jnp.sin(fq_ref[:, 0:1] * (h + b1_ref[...]))
    h = jnp.sin(fq_ref[:, 1:2] * (dot(w2t_ref[...], h) + b2_ref[...]))
    h = jnp.sin(fq_ref[:, 2:3] * (dot(w3t_ref[...], h) + b3_ref[...]))
    ht = h.T.astype(BF16)
    n_col = lags((tn, 1), 0)
    decay = jnp.exp(-(n_col.astype(F32) * (1.0 / (l - 1))) * dl_ref[...])
    decay_b = jnp.where(n_col == 0, 0.0, decay)
    for od in range(2 * HYENA_ORDER):
        f = jnp.dot(ht, w4_ref[:, od * c:(od + 1) * c].astype(BF16), preferred_element_type=F32)
        f = (f * (decay_b if od % 2 else decay)).astype(BF16)
        for j in range(rj):
            out_ref[od, j] = jnp.dot(m_ref[j], f[j * half:(j + 1) * half],
                                     preferred_element_type=F32).astype(BF16)


def _filter_stage1(l, w1, b1, w2, b2, w3, b3, freq, w4, mats, *, na, nb):
    half = na // 2
    rj = max(1, FILTER_TILE // half)
    bands = FILTER_BANDS
    fr = jnp.asarray(np.linspace(1e-4, bands - 1, bands, dtype=np.float32)[:, None])
    max_decay = math.log(DECAY_TARGET) / FAST_DECAY_PCT
    min_decay = math.log(DECAY_TARGET) / SLOW_DECAY_PCT
    deltas = jnp.asarray(np.abs(np.linspace(min_decay, max_decay, HYENA_WIDTH, dtype=np.float32))[None, :])
    full = lambda arr: pl.BlockSpec(arr.shape, lambda i: (0,) * arr.ndim)
    args = (w1.T, b1[:, None], w2.T, b2[:, None], w3.T, b3[:, None], freq.T, w4, fr, deltas)
    return pl.pallas_call(
        functools.partial(_filter_kernel, rj=rj, half=half, nb=nb, l=l),
        grid=(nb // rj,),
        in_specs=[full(a) for a in args] + [pl.BlockSpec((rj, na, half), lambda i: (i, 0, 0))],
        out_specs=pl.BlockSpec((2 * HYENA_ORDER, rj, na, HYENA_WIDTH), lambda i: (0, i, 0, 0)),
        out_shape=jax.ShapeDtypeStruct((2 * HYENA_ORDER, nb, na, HYENA_WIDTH), BF16),
        compiler_params=_cparams("parallel"),
        name=f"filters_dft1_l{l}",
    )(*args, mats)


def _fft_plan(l):
    n = 2 * l
    na = 256 if n >= 32768 else (128 if n >= 8192 else 64)
    return na, n // na


@functools.lru_cache(maxsize=None)
def _dft_tables(l):
    na, nb = _fft_plan(l)
    n = na * nb
    ka = np.arange(na // 2, dtype=np.int64)
    nas = np.arange(na // 2, dtype=np.int64)
    nbs = np.arange(nb, dtype=np.int64)
    num = ((2 * ka[None, :, None] + 1) * (nb * nas[None, None, :] + nbs[:, None, None])) % (2 * n)
    theta = num.astype(np.float64) * (math.pi / n)
    fwd = np.concatenate([np.cos(theta), -np.sin(theta)], axis=1)
    thetat = np.swapaxes(theta, 1, 2)
    inv = (2.0 / n) * np.concatenate([np.cos(thetat), -np.sin(thetat)], axis=2)
    phi = ((nbs[:, None] * nbs[None, :]) % nb).astype(np.float64) * (2.0 * math.pi / nb)
    cc, ss = np.cos(phi), np.sin(phi)
    e2 = np.block([[cc, ss], [-ss, cc]])
    e2i = np.block([[cc, -ss], [ss, cc]])
    e2fb = np.concatenate([e2, np.block([[cc, ss], [ss, -cc]])], axis=1)
    cast = lambda arr: np.ascontiguousarray(arr.astype(np.float32))
    return cast(fwd), cast(inv), cast(e2), cast(e2i), cast(e2fb)


def _rows_per_step(half, nb):
    want = max(1, STEP_INPUT_BYTES // (half * SLAB * HYENA_WIDTH * 2))
    return min(nb, SLAB * want)


def _s1_kernel(src_ref, m_ref, out_ref):
    xt = jnp.swapaxes(src_ref[0], 0, 1)
    for j in range(xt.shape[0]):
        out_ref[0, j] = jnp.dot(m_ref[j], xt[j], preferred_element_type=F32).astype(BF16)


def _stage1(src, mats, *, na, nb):
    bx, l, c = src.shape
    half = na // 2
    rj = _rows_per_step(half, nb)
    return pl.pallas_call(
        _s1_kernel,
        grid=(nb // rj, bx),
        in_specs=[pl.BlockSpec((1, half, rj, c), lambda j, bi: (bi, 0, j, 0)),
                  pl.BlockSpec((rj, na, half), lambda j, bi: (j, 0, 0))],
        out_specs=pl.BlockSpec((1, rj, na, c), lambda j, bi: (bi, j, 0, 0)),
        out_shape=jax.ShapeDtypeStruct((bx, nb, na, c), BF16),
        compiler_params=_cparams("parallel", "arbitrary"),
        name=f"dft1_l{l}",
    )(src.reshape(bx, half, nb, c), mats)


def _s2f_kernel(yf_ref, yb_ref, e2_ref, h_ref, *, nb):
    c = HYENA_WIDTH
    ft = jnp.swapaxes(yf_ref[0].reshape(nb, 2 * SLAB, -1), 0, 1)
    bt = jnp.swapaxes(yb_ref[0].reshape(nb, 2 * SLAB, -1), 0, 1)
    chunk = SLAB if nb <= 64 else SLAB // 2
    for t0 in range(0, SLAB, chunk):
        ts = range(t0, t0 + chunk)
        rows = [jnp.concatenate([src[part * SLAB + t] for t in ts], axis=1)
                for src in (ft, bt) for part in range(2)]
        y = jnp.dot(e2_ref[...], jnp.concatenate(rows, axis=0), preferred_element_type=F32).astype(BF16)
        for k, t in enumerate(ts):
            h_ref[0, t] = y[:, k * c:(k + 1) * c]


def _filter_spectrum(y1, e2, *, na, nb):
    c = HYENA_WIDTH
    y5 = y1.reshape(2 * HYENA_ORDER, nb, 2, na // 2, c)
    spec = lambda d: pl.BlockSpec((1, nb, 2, SLAB, c), lambda o, k: (2 * o + d, 0, 0, k, 0))
    return pl.pallas_call(
        functools.partial(_s2f_kernel, nb=nb),
        grid=(HYENA_ORDER, na // 2 // SLAB),
        in_specs=[spec(0), spec(1), pl.BlockSpec((2 * nb, 4 * nb), lambda o, k: (0, 0))],
        out_specs=pl.BlockSpec((1, SLAB, 2 * nb, c), lambda o, k: (o, k, 0, 0)),
        out_shape=jax.ShapeDtypeStruct((HYENA_ORDER, na // 2, 2 * nb, c), BF16),
        compiler_params=_cparams("parallel", "arbitrary"),
        name=f"filter_spec_n{na * nb}",
    )(y5, y5, e2)


def _s2_kernel(y_ref, h_ref, e2_ref, e2i_ref, out_ref, ot_ref, *, nb):
    c = HYENA_WIDTH
    yt = jnp.swapaxes(y_ref[0].reshape(nb, 2 * SLAB, -1), 0, 1)
    chunk = SLAB if nb <= 64 else SLAB // 2
    for t0 in range(0, SLAB, chunk):
        ts = range(t0, t0 + chunk)
        d = jnp.concatenate([jnp.concatenate([yt[part * SLAB + t] for t in ts], axis=1)
                             for part in range(2)], axis=0)
        y = jnp.dot(e2_ref[...], d, preferred_element_type=F32)
        h = jnp.concatenate([h_ref[0, t] for t in ts], axis=1).astype(F32)
        yr, yi, hr, hi = y[:nb], y[nb:], h[:nb], h[nb:]
        p = jnp.concatenate([yr * hr - yi * hi, yr * hi + yi * hr], axis=0).astype(BF16)
        o = jnp.dot(e2i_ref[...], p, preferred_element_type=F32).astype(BF16)
        for k, t in enumerate(ts):
            ot_ref[t] = o[:nb, k * c:(k + 1) * c]
            ot_ref[SLAB + t] = o[nb:, k * c:(k + 1) * c]
    out_ref[0] = jnp.swapaxes(ot_ref[...], 0, 1).reshape(nb, 2, SLAB, -1)


def _stage2(y1, hspec, order, e2, e2i, *, na, nb):
    bx = y1.shape[0]
    c = HYENA_WIDTH
    y5 = y1.reshape(bx, nb, 2, na // 2, c)
    yspec = pl.BlockSpec((1, nb, 2, SLAB, c), lambda k, bi: (bi, 0, 0, k, 0))
    mat = pl.BlockSpec((2 * nb, 2 * nb), lambda k, bi: (0, 0))
    out = pl.pallas_call(
        functools.partial(_s2_kernel, nb=nb),
        grid=(na // 2 // SLAB, bx),
        in_specs=[yspec, pl.BlockSpec((1, SLAB, 2 * nb, c), lambda k, bi: (order, k, 0, 0)), mat, mat],
        out_specs=yspec,
        out_shape=jax.ShapeDtypeStruct((bx, nb, 2, na // 2, c), BF16),
        scratch_shapes=[pltpu.VMEM((2 * SLAB, nb, c), BF16)],
        compiler_params=_cparams("parallel", "arbitrary"),
        name=f"dft2_n{na * nb}",
    )(y5, hspec, e2, e2i)
    return out.reshape(bx, nb, na, c)


def _s1inv_chain_kernel(y_ref, g_ref, z_ref, gate_ref, d_ref, m_ref, znew_ref, y1_ref):
    zt = jnp.swapaxes(z_ref[0], 0, 1)
    gt = jnp.swapaxes(gate_ref[0], 0, 1)
    for j in range(zt.shape[0]):
        conv = jnp.dot(g_ref[j], y_ref[0, j], preferred_element_type=F32)
        zn = (gt[j].astype(F32) * (conv + d_ref[...] * zt[j].astype(F32))).astype(BF16)
        znew_ref[0, j] = zn
        y1_ref[0, j] = jnp.dot(m_ref[j], zn, preferred_element_type=F32).astype(BF16)


def _s1inv_last_kernel(y_ref, g_ref, z_ref, gate_ref, d_ref, znew_ref, zt_ref):
    gt = jnp.swapaxes(gate_ref[0], 0, 1)
    for j in range(gt.shape[0]):
        conv = jnp.dot(g_ref[j], y_ref[0, j], preferred_element_type=F32)
        zt_ref[j] = (gt[j].astype(F32) * (conv + d_ref[...] * z_ref[0, j].astype(F32))).astype(BF16)
    znew_ref[0] = jnp.swapaxes(zt_ref[...], 0, 1)


def _stage1_inverse(y3, ginv, z, gate, d, mats, *, na, nb):
    bx = y3.shape[0]
    c = HYENA_WIDTH
    half = na // 2
    l = half * nb
    rj = _rows_per_step(half, nb)
    nat = pl.BlockSpec((1, half, rj, c), lambda j, bi: (bi, 0, j, 0))
    tr = lambda rows: pl.BlockSpec((1, rj, rows, c), lambda j, bi: (bi, j, 0, 0))
    gspec = pl.BlockSpec((rj, half, na), lambda j, bi: (j, 0, 0))
    dspec = pl.BlockSpec((1, c), lambda j, bi: (0, 0))
    gate4 = gate.reshape(bx, half, nb, c)
    if mats is not None:
        return pl.pallas_call(
            _s1inv_chain_kernel,
            grid=(nb // rj, bx),
            in_specs=[tr(na), gspec, nat, nat, dspec, pl.BlockSpec((rj, na, half), lambda j, bi: (j, 0, 0))],
            out_specs=[tr(half), tr(na)],
            out_shape=[jax.ShapeDtypeStruct((bx, nb, half, c), BF16),
                       jax.ShapeDtypeStruct((bx, nb, na, c), BF16)],
            compiler_params=_cparams("parallel", "arbitrary"),
            name=f"idft1_chain_l{l}",
        )(y3, ginv, z.reshape(bx, half, nb, c), gate4, d, mats)
    out = pl.pallas_call(
        _s1inv_last_kernel,
        grid=(nb // rj, bx),
        in_specs=[tr(na), gspec, tr(half), nat, dspec],
        out_specs=nat,
        out_shape=jax.ShapeDtypeStruct((bx, half, nb, c), BF16),
        scratch_shapes=[pltpu.VMEM((rj, half, c), BF16)],
        compiler_params=_cparams("parallel", "arbitrary"),
        name=f"idft1_last_l{l}",
    )(y3, ginv, z, gate4, d)
    return out.reshape(bx, l, c)


def _hyena(z0, x1, x2, filter_params, hyena_d):
    bx, l, c = z0.shape
    na, nb = _fft_plan(l)
    m_fwd, ginv, e2, e2i, e2fb = (jnp.asarray(t).astype(BF16) for t in _dft_tables(l))
    hspec = _filter_spectrum(_filter_stage1(l, *filter_params, m_fwd, na=na, nb=nb), e2fb, na=na, nb=nb)
    y1 = _stage1(z0, m_fwd, na=na, nb=nb)
    y3 = _stage2(y1, hspec, 0, e2, e2i, na=na, nb=nb)
    z1, y1 = _stage1_inverse(y3, ginv, z0, x1, hyena_d[0:1], m_fwd, na=na, nb=nb)
    y3 = _stage2(y1, hspec, 1, e2, e2i, na=na, nb=nb)
    return _stage1_inverse(y3, ginv, z1, x2, hyena_d[1:2], None, na=na, nb=nb)


def _final_kernel(attn_ref, ga_ref, z_ref, gh_ref, x_ref, ag_ref, hg_ref, wo_ref, lg_ref, lb_ref, y_ref):
    def rms(v, g):
        return v * lax.rsqrt(jnp.mean(v * v, axis=-1, keepdims=True) + RMS_EPS) * g

    ma = (rms(attn_ref[0].astype(F32), ag_ref[...]) * ga_ref[0].astype(F32)).astype(BF16)
    mh = (rms(z_ref[0].astype(F32), hg_ref[...]) * gh_ref[0].astype(F32)).astype(BF16)
    a = ATTN_WIDTH
    out = (jnp.dot(ma, wo_ref[0:a, :], preferred_element_type=F32)
           + jnp.dot(mh, wo_ref[a:, :], preferred_element_type=F32))
    h = ((2.0 * DEPTH) ** 0.25) * x_ref[0] + out
    mu = jnp.mean(h, axis=-1, keepdims=True)
    hc = h - mu
    var = jnp.mean(hc * hc, axis=-1, keepdims=True)
    y_ref[0] = hc * lax.rsqrt(var + LN_EPS) * lg_ref[...] + lb_ref[...]


def _final(attn, qkvg, z, gh, x, attn_g, hyena_g, w_out_bf, ln_g, ln_b, *, tt):
    b, l, d = x.shape
    a = ATTN_WIDTH
    tok = lambda w, col=0: pl.BlockSpec((1, tt, w), lambda bi, i: (bi, i, col))
    vec = lambda w: pl.BlockSpec((1, w), lambda bi, i: (0, 0))
    return pl.pallas_call(
        _final_kernel,
        grid=(b, l // tt),
        in_specs=[tok(a), tok(a, 3), tok(a), tok(a), tok(d),
                  vec(a), vec(a), pl.BlockSpec((d, d), lambda bi, i: (0, 0)), vec(d), vec(d)],
        out_specs=tok(d),
        out_shape=jax.ShapeDtypeStruct((b, l, d), F32),
        compiler_params=_cparams("parallel", "arbitrary"),
        name="outproj_ln",
    )(attn, qkvg, z, gh, x, attn_g[None, :], hyena_g[None, :], w_out_bf, ln_g[None, :], ln_b[None, :])


def _layer(x, w_in_bf, conv_w, conv_b, filt_w1, filt_b1, filt_w2, filt_b2, filt_w3, filt_b3,
           filt_freq, filt_w4, hyena_d, attn_norm_g, hyena_norm_g, w_out_bf, ln_g, ln_b):
    b, l, _ = x.shape
    tm = min(1024, l)
    qkvg, z0, x1, x2, gh = _inproj(x, w_in_bf, conv_w, conv_b[None, :], tm=tm)
    attn = _attention(qkvg)
    z = _hyena(z0, x1, x2, (filt_w1, filt_b1, filt_w2, filt_b2, filt_w3, filt_b3, filt_freq, filt_w4), hyena_d)
    return _final(attn, qkvg, z, gh, x, attn_norm_g, hyena_norm_g, w_out_bf, ln_g, ln_b, tt=tm)


def kernel(x_prompt, x_sample, w_in, conv_w, conv_b, filt_w1, filt_b1, filt_w2, filt_b2, filt_w3,
           filt_b3, filt_freq, filt_w4, hyena_d, attn_norm_g, hyena_norm_g, w_out, ln_g, ln_b):
    def trunk(x):
        for i in range(DEPTH):
            x = _layer(x, w_in[i].astype(BF16), conv_w[i], conv_b[i], filt_w1[i], filt_b1[i], filt_w2[i],
                       filt_b2[i], filt_w3[i], filt_b3[i], filt_freq[i], filt_w4[i], hyena_d[i],
                       attn_norm_g[i], hyena_norm_g[i], w_out[i].astype(BF16), ln_g[i], ln_b[i])
        return x
    return (trunk(x_prompt), trunk(x_sample))
```

```python
import functools
import math

import numpy as np
import jax
import jax.numpy as jnp
from jax import lax
from jax.experimental import pallas as pl
from jax.experimental.pallas import tpu as pltpu

F32 = jnp.float32
BF16 = jnp.bfloat16

D_MODEL = 1024
ATTN_WIDTH = 512
HYENA_WIDTH = 512
HEAD_DIM = 64
N_HEADS = 8
HALF_WINDOW = 64
HYENA_ORDER = 2
FILTER_BANDS = 16
FILTER_HIDDEN = 64
FAST_DECAY_PCT = 0.3
SLOW_DECAY_PCT = 1.5
DECAY_TARGET = 1e-2
PROJ_WIDTH = 4096
LN_EPS = 1e-5
RMS_EPS = 1e-6
NEG_INF = -1e30
DEPTH = 1

Q_BLOCK = 128
KEY_WINDOW = 256
SLAB = 16
HEAD_GROUP = 4
GROUP_WIDTH = HEAD_GROUP * HEAD_DIM
COLUMN_CHUNK = 256
LOG2_E = math.log2(math.e)
Q_SCALE = HEAD_DIM ** -0.5 * LOG2_E
VMEM_LIMIT = 56 * 1024 * 1024
FILTER_TILE = 512
STEP_INPUT_BYTES = 2 * 1024 * 1024


def _cparams(*sem):
    return pltpu.CompilerParams(dimension_semantics=sem, vmem_limit_bytes=VMEM_LIMIT)


def _silu(g):
    return g / (1.0 + jnp.exp(-g))


def _inproj_kernel(xp_ref, x_ref, xn_ref, w_ref, cw_ref, cb_ref,
                   qkvg_ref, z0_ref, x1_ref, x2_ref, gh_ref, *, tm, n_tiles):
    i = pl.program_id(1)
    a, c = ATTN_WIDTH, HYENA_WIDTH
    xb = x_ref[0].astype(BF16)

    def proj(lo, hi, lhs=xb):
        return jnp.dot(lhs, w_ref[:, lo:hi], preferred_element_type=F32)

    halo = jnp.concatenate([xp_ref[0, 0], xn_ref[0, 0]], axis=0).astype(BF16)
    xext = jnp.concatenate([xb, halo], axis=0)
    cw2 = COLUMN_CHUNK
    row = lax.broadcasted_iota(jnp.int32, (tm, cw2), 0)
    has_prev, has_next = (i > 0).astype(F32), (i < n_tiles - 1).astype(F32)
    for part, dst in enumerate((z0_ref, x1_ref, x2_ref)):
        for c0 in range(0, c, cw2):
            lo = 4 * a + part * c + c0
            pext = proj(lo, lo + cw2, xext)
            p = pext[:tm]
            below = jnp.where(row == 0, pext[tm + 7:tm + 8] * has_prev, pltpu.roll(p, 1, 0))
            above = jnp.where(row == tm - 1, pext[tm + 8:tm + 9] * has_next, pltpu.roll(p, tm - 1, 0))
            cs = slice(part * c + c0, part * c + c0 + cw2)
            dst[0, :, c0:c0 + cw2] = (cb_ref[:, cs] + below * cw_ref[0:1, cs] + p * cw_ref[1:2, cs]
                                      + above * cw_ref[2:3, cs]).astype(BF16)

    for c0 in range(0, a, cw2):
        qkvg_ref[0, :, 3 * a + c0:3 * a + c0 + cw2] = _silu(proj(3 * a + c0, 3 * a + c0 + cw2)).astype(BF16)
        gh_ref[0, :, c0:c0 + cw2] = _silu(proj(4 * a + 3 * c + c0, 4 * a + 3 * c + c0 + cw2)).astype(BF16)
        qkvg_ref[0, :, c0:c0 + cw2] = (proj(c0, c0 + cw2) * Q_SCALE).astype(BF16)
    for c0 in range(a, 3 * a, cw2):
        qkvg_ref[0, :, c0:c0 + cw2] = proj(c0, c0 + cw2).astype(BF16)


def _inproj(x, w_bf, conv_w, conv_b, *, tm):
    b, l, d = x.shape
    n_tiles = l // tm
    x8 = x.reshape(b, l // 8, 8, d)
    r = tm // 8
    n8 = l // 8
    bf_out = lambda w: jax.ShapeDtypeStruct((b, l, w), BF16)
    row_spec = lambda w: pl.BlockSpec((1, tm, w), lambda bi, i: (bi, i, 0))
    return pl.pallas_call(
        functools.partial(_inproj_kernel, tm=tm, n_tiles=n_tiles),
        grid=(b, n_tiles),
        in_specs=[
            pl.BlockSpec((1, 1, 8, d), lambda bi, i: (bi, jnp.maximum(i * r - 1, 0), 0, 0)),
            pl.BlockSpec((1, tm, d), lambda bi, i: (bi, i, 0)),
            pl.BlockSpec((1, 1, 8, d), lambda bi, i: (bi, jnp.minimum((i + 1) * r, n8 - 1), 0, 0)),
            pl.BlockSpec((d, PROJ_WIDTH), lambda bi, i: (0, 0)),
            pl.BlockSpec((3, 3 * HYENA_WIDTH), lambda bi, i: (0, 0)),
            pl.BlockSpec((1, 3 * HYENA_WIDTH), lambda bi, i: (0, 0)),
        ],
        out_specs=[row_spec(4 * ATTN_WIDTH), row_spec(HYENA_WIDTH), row_spec(HYENA_WIDTH),
                   row_spec(HYENA_WIDTH), row_spec(HYENA_WIDTH)],
        out_shape=[bf_out(4 * ATTN_WIDTH), bf_out(HYENA_WIDTH), bf_out(HYENA_WIDTH),
                   bf_out(HYENA_WIDTH), bf_out(HYENA_WIDTH)],
        compiler_params=_cparams("parallel", "arbitrary"),
        name="inproj",
    )(x8, x, x8, w_bf, conv_w, conv_b)


def _band_bias_tables():
    qi = np.arange(Q_BLOCK)[:, None]
    kj = np.arange(KEY_WINDOW)[None, :]
    rel_band = kj - HALF_WINDOW - qi
    rel_d4 = 4 * ((kj % 64) - 16 - (qi % 32)) + (kj // 64 - qi // 32)
    slopes = np.asarray([2.0 ** (-8.0 * (i + 1) / N_HEADS) for i in range(N_HEADS)], np.float32)
    out = []
    for rel, dil in ((rel_band, 1), (rel_d4, 4), (rel_band, 16)):
        rel = np.abs(rel)
        bias = -(LOG2_E * slopes[:, None, None]) * (rel * dil).astype(np.float32)[None]
        out.append(np.where(rel[None] <= HALF_WINDOW, bias, NEG_INF).astype(np.float32))
    return np.stack(out)


def _attn_kernel(q_ref, kp_ref, k_ref, kn_ref, vp_ref, v_ref, vn_ref, bias_ref, o_ref,
                 qt_ref, kt_ref, vt_ref, acc_ref, m_ref, l_ref, on_ref, ln_ref,
                 *, ti, s16, seq_len):
    i0 = pl.program_id(1) * ti
    hw = HALF_WINDOW
    dn_t = (((1,), (1,)), ((), ()))

    qt_ref[...] = jnp.swapaxes(q_ref[0], 0, 1)
    kt_ref[...] = jnp.swapaxes(jnp.concatenate([kp_ref[0], k_ref[0], kn_ref[0]], axis=0), 0, 1)
    vt_ref[...] = jnp.swapaxes(jnp.concatenate([vp_ref[0], v_ref[0], vn_ref[0]], axis=0), 0, 1)

    col = lax.broadcasted_iota(jnp.int32, (1, KEY_WINDOW), 1)

    pair = 2 * HEAD_DIM
    first = lax.broadcasted_iota(jnp.int32, (1, pair), 1) < HEAD_DIM
    ones_cols = jnp.ones((KEY_WINDOW, pair), BF16)

    def tile(q, kw, vw, pat, kvalid):
        ms, ls, accs = [], [], []
        for pr in range(HEAD_GROUP // 2):
            ps = slice(pr * pair, (pr + 1) * pair)
            qp, kp, vext = q[:, ps], kw[:, ps], jnp.concatenate([vw[:, ps], ones_cols], axis=1)
            halves = []
            for hh in range(2):
                own = first if hh == 0 else jnp.logical_not(first)
                s = lax.dot_general(jnp.where(own, qp, jnp.zeros_like(qp)), kp, dn_t,
                                    preferred_element_type=F32)
                s = s + bias_ref[pat, 2 * pr + hh]
                if kvalid is not None:
                    s = jnp.where(kvalid, s, NEG_INF)
                m = jnp.max(s, axis=-1, keepdims=True)
                pv = jnp.dot(jnp.exp2(s - m).astype(BF16), vext, preferred_element_type=F32)
                halves.append((m, pv))
            (m0, pv0), (m1, pv1) = halves
            ms.append(jnp.where(first, m0, m1))
            accs.append(jnp.where(first, pv0[:, :pair], pv1[:, :pair]))
            ls.append(jnp.where(first, pv0[:, pair:], pv1[:, pair:]))
        cat = lambda parts: jnp.concatenate(parts, axis=1)
        return cat(ms), cat(ls), cat(accs)

    for r in range(SLAB):
        for js in range(0, ti, Q_BLOCK):
            i16 = i0 + js - hw + col
            m, l, acc = tile(qt_ref[r, js:js + Q_BLOCK, :], kt_ref[r, js:js + KEY_WINDOW, :],
                             vt_ref[r, js:js + KEY_WINDOW, :], 2, (i16 >= 0) & (i16 < s16))
            m_ref[r, js:js + Q_BLOCK, :] = m
            l_ref[r, js:js + Q_BLOCK, :] = l
            acc_ref[r, js:js + Q_BLOCK, :] = acc

    for r4 in range(4):
        for is_ in range(0, ti, 32):
            ks = is_ + hw - 16
            q = jnp.concatenate([qt_ref[r4 + 4 * q4, is_:is_ + 32, :] for q4 in range(4)], axis=0)
            kw = jnp.concatenate([kt_ref[r4 + 4 * q4, ks:ks + 64, :] for q4 in range(4)], axis=0)
            vw = jnp.concatenate([vt_ref[r4 + 4 * q4, ks:ks + 64, :] for q4 in range(4)], axis=0)
            i4 = i0 + is_ - 16 + (col % 64)
            inside = is_ - 16 >= 0 and is_ + 48 <= ti
            m4, l4, a4 = tile(q, kw, vw, 1, None if inside else (i4 >= 0) & (i4 < s16))
            for q4 in range(4):
                rows = slice(q4 * 32, (q4 + 1) * 32)
                at = (r4 + 4 * q4, slice(is_, is_ + 32), slice(None))
                m_old, l_old, a_old = m_ref[at], l_ref[at], acc_ref[at]
                m_new = jnp.maximum(m_old, m4[rows])
                alpha, beta = jnp.exp2(m_old - m_new), jnp.exp2(m4[rows] - m_new)
                m_ref[at] = m_new
                l_ref[at] = l_old * alpha + l4[rows] * beta
                acc_ref[at] = a_old * alpha + a4[rows] * beta

    on_ref[...] = jnp.swapaxes((acc_ref[...] / l_ref[...]).astype(BF16), 0, 1)
    ln_ref[...] = jnp.swapaxes(m_ref[...] + jnp.log2(l_ref[...]), 0, 1)

    def window(prev_ref, ref, next_ref, s8):
        lo, hi = s8 - 4, s8 + 12
        parts = []
        if lo < 0:
            parts.append(prev_ref[0, hw + lo:hw])
        parts.append(ref[0, max(lo, 0):min(hi, ti)])
        if hi > ti:
            parts.append(next_ref[0, 0:hi - ti])
        w = parts[0] if len(parts) == 1 else jnp.concatenate(parts, axis=0)
        return w.reshape(KEY_WINDOW, GROUP_WIDTH)

    for s8 in range(0, ti, 8):
        q = q_ref[0, s8:s8 + 8].reshape(Q_BLOCK, GROUP_WIDTH)
        kw = window(kp_ref, k_ref, kn_ref, s8)
        vw = window(vp_ref, v_ref, vn_ref, s8)
        tok = SLAB * (i0 + s8 - 4) + col
        inside = s8 - 4 >= 0 and s8 + 12 <= ti
        m1, l1, a1 = tile(q, kw, vw, 0, None if inside else (tok >= 0) & (tok < seq_len))
        lse_p = ln_ref[s8:s8 + 8].reshape(Q_BLOCK, GROUP_WIDTH)
        o_p = on_ref[s8:s8 + 8].reshape(Q_BLOCK, GROUP_WIDTH).astype(F32)
        m = jnp.maximum(m1, lse_p)
        alpha, beta = jnp.exp2(m1 - m), jnp.exp2(lse_p - m)
        out = (a1 * alpha + o_p * beta) / (l1 * alpha + beta)
        o_ref[0, s8:s8 + 8] = out.astype(BF16).reshape(8, SLAB, GROUP_WIDTH)


def _attention(qkvg):
    b, l, w4 = qkvg.shape
    s16 = l // SLAB
    ti = min(128, s16)
    hw = HALF_WINDOW
    gw = GROUP_WIDTH
    view = qkvg.reshape(b, s16, SLAB, w4)
    ng = ATTN_WIDTH // gw
    n_halo = s16 // hw
    rq = ti // hw

    def main(part):
        return pl.BlockSpec((1, ti, SLAB, gw), lambda bi, i, g: (bi, i, 0, part * ng + g))

    def prev(part):
        return pl.BlockSpec((1, hw, SLAB, gw),
                            lambda bi, i, g: (bi, jnp.maximum(i * rq - 1, 0), 0, part * ng + g))

    def nxt(part):
        return pl.BlockSpec((1, hw, SLAB, gw),
                            lambda bi, i, g: (bi, jnp.minimum((i + 1) * rq, n_halo - 1), 0, part * ng + g))

    vm = lambda shape, dt: pltpu.VMEM(shape, dt)
    out = pl.pallas_call(
        functools.partial(_attn_kernel, ti=ti, s16=s16, seq_len=l),
        grid=(b, s16 // ti, ng),
        in_specs=[main(0), prev(1), main(1), nxt(1), prev(2), main(2), nxt(2),
                  pl.BlockSpec((3, HEAD_GROUP, Q_BLOCK, KEY_WINDOW), lambda bi, i, g: (0, g, 0, 0))],
        out_specs=pl.BlockSpec((1, ti, SLAB, gw), lambda bi, i, g: (bi, i, 0, g)),
        out_shape=jax.ShapeDtypeStruct((b, s16, SLAB, ATTN_WIDTH), BF16),
        scratch_shapes=[vm((SLAB, ti, gw), BF16), vm((SLAB, ti + 2 * hw, gw), BF16),
                        vm((SLAB, ti + 2 * hw, gw), BF16),
                        vm((SLAB, ti, gw), F32), vm((SLAB, ti, gw), F32), vm((SLAB, ti, gw), F32),
                        vm((ti, SLAB, gw), BF16), vm((ti, SLAB, gw), F32)],
        compiler_params=_cparams("parallel", "parallel", "arbitrary"),
        name="band_attn",
    )(view, view, view, view, view, view, view, jnp.asarray(_band_bias_tables()))
    return out.reshape(b, l, ATTN_WIDTH)


def _filter_kernel(w1t_ref, b1_ref, w2t_ref, b2_ref, w3t_ref, b3_ref, fq_ref, w4_ref, fr_ref, dl_ref,
                   m_ref, out_ref, *, rj, half, nb, l):
    hp = lax.Precision.HIGHEST
    c = HYENA_WIDTH
    nbd = FILTER_BANDS
    tn = rj * half
    j0 = pl.program_id(0) * rj

    def lags(shape, axis):
        k = lax.broadcasted_iota(jnp.int32, shape, axis)
        return nb * (k % half) + j0 + k // half

    pos = lags((1, tn), 1).astype(F32)
    t = pos * (1.0 / (l - 1))
    ang = fr_ref[...] * ((2.0 * math.pi / l) * pos)
    dot = lambda a, b: jnp.dot(a, b, precision=hp, preferred_element_type=F32)
    h = (w1t_ref[:, 0:1] * t + dot(w1t_ref[:, 1:1 + nbd], jnp.cos(ang))
         - dot(w1t_ref[:, 1 + nbd:1 + 2 * nbd], jnp.sin(ang)))
    h = jnp.sin(fq_ref[:, 0:1] * (h + b1_ref[...]))
    h = jnp.sin(fq_ref[:, 1:2] * (dot(w2t_ref[...], h) + b2_ref[...]))
    h = jnp.sin(fq_ref[:, 2:3] * (dot(w3t_ref[...], h) + b3_ref[...]))
    ht = h.T.astype(BF16)
    n_col = lags((tn, 1), 0)
    decay = jnp.exp(-(n_col.astype(F32) * (1.0 / (l - 1))) * dl_ref[...])
    decay_b = jnp.where(n_col == 0, 0.0, decay)
    for od in range(2 * HYENA_ORDER):
        f = jnp.dot(ht, w4_ref[:, od * c:(od + 1) * c].astype(BF16), preferred_element_type=F32)
        f = (f * (decay_b if od % 2 else decay)).astype(BF16)
        for j in range(rj):
            out_ref[od, j] = jnp.dot(m_ref[j], f[j * half:(j + 1) * half],
                                     preferred_element_type=F32).astype(BF16)


def _filter_stage1(l, w1, b1, w2, b2, w3, b3, freq, w4, mats, *, na, nb):
    half = na // 2
    rj = max(1, FILTER_TILE // half)
    bands = FILTER_BANDS
    fr = jnp.asarray(np.linspace(1e-4, bands - 1, bands, dtype=np.float32)[:, None])
    max_decay = math.log(DECAY_TARGET) / FAST_DECAY_PCT
    min_decay = math.log(DECAY_TARGET) / SLOW_DECAY_PCT
    deltas = jnp.asarray(np.abs(np.linspace(min_decay, max_decay, HYENA_WIDTH, dtype=np.float32))[None, :])
    full = lambda arr: pl.BlockSpec(arr.shape, lambda i: (0,) * arr.ndim)
    args = (w1.T, b1[:, None], w2.T, b2[:, None], w3.T, b3[:, None], freq.T, w4, fr, deltas)
    return pl.pallas_call(
        functools.partial(_filter_kernel, rj=rj, half=half, nb=nb, l=l),
        grid=(nb // rj,),
        in_specs=[full(a) for a in args] + [pl.BlockSpec((rj, na, half), lambda i: (i, 0, 0))],
        out_specs=pl.BlockSpec((2 * HYENA_ORDER, rj, na, HYENA_WIDTH), lambda i: (0, i, 0, 0)),
        out_shape=jax.ShapeDtypeStruct((2 * HYENA_ORDER, nb, na, HYENA_WIDTH), BF16),
        compiler_params=_cparams("parallel"),
        name=f"filters_dft1_l{l}",
    )(*args, mats)


def _fft_plan(l):
    n = 2 * l
    na = 256 if n >= 32768 else (128 if n >= 8192 else 64)
    return na, n // na


@functools.lru_cache(maxsize=None)
def _dft_tables(l):
    na, nb = _fft_plan(l)
    n = na * nb
    ka = np.arange(na // 2, dtype=np.int64)
    nas = np.arange(na // 2, dtype=np.int64)
    nbs = np.arange(nb, dtype=np.int64)
    num = ((2 * ka[None, :, None] + 1) * (nb * nas[None, None, :] + nbs[:, None, None])) % (2 * n)
    theta = num.astype(np.float64) * (math.pi / n)
    fwd = np.concatenate([np.cos(theta), -np.sin(theta)], axis=1)
    thetat = np.swapaxes(theta, 1, 2)
    inv = (2.0 / n) * np.concatenate([np.cos(thetat), -np.sin(thetat)], axis=2)
    phi = ((nbs[:, None] * nbs[None, :]) % nb).astype(np.float64) * (2.0 * math.pi / nb)
    cc, ss = np.cos(phi), np.sin(phi)
    e2 = np.block([[cc, ss], [-ss, cc]])
    e2i = np.block([[cc, -ss], [ss, cc]])
    e2fb = np.concatenate([e2, np.block([[cc, ss], [ss, -cc]])], axis=1)
    cast = lambda arr: np.ascontiguousarray(arr.astype(np.float32))
    return cast(fwd), cast(inv), cast(e2), cast(e2i), cast(e2fb)


def _rows_per_step(half, nb):
    want = max(1, STEP_INPUT_BYTES // (half * SLAB * HYENA_WIDTH * 2))
    return min(nb, SLAB * want)


def _s1_kernel(src_ref, m_ref, out_ref):
    xt = jnp.swapaxes(src_ref[0], 0, 1)
    for j in range(xt.shape[0]):
        out_ref[0, j] = jnp.dot(m_ref[j], xt[j], preferred_element_type=F32).astype(BF16)


def _stage1(src, mats, *, na, nb):
    bx, l, c = src.shape
    half = na // 2
    rj = _rows_per_step(half, nb)
    return pl.pallas_call(
        _s1_kernel,
        grid=(nb // rj, bx),
        in_specs=[pl.BlockSpec((1, half, rj, c), lambda j, bi: (bi, 0, j, 0)),
                  pl.BlockSpec((rj, na, half), lambda j, bi: (j, 0, 0))],
        out_specs=pl.BlockSpec((1, rj, na, c), lambda j, bi: (bi, j, 0, 0)),
        out_shape=jax.ShapeDtypeStruct((bx, nb, na, c), BF16),
        compiler_params=_cparams("parallel", "arbitrary"),
        name=f"dft1_l{l}",
    )(src.reshape(bx, half, nb, c), mats)


def _s2f_kernel(yf_ref, yb_ref, e2_ref, h_ref, *, nb):
    c = HYENA_WIDTH
    ft = jnp.swapaxes(yf_ref[0].reshape(nb, 2 * SLAB, -1), 0, 1)
    bt = jnp.swapaxes(yb_ref[0].reshape(nb, 2 * SLAB, -1), 0, 1)
    chunk = SLAB if nb <= 64 else SLAB // 2
    for t0 in range(0, SLAB, chunk):
        ts = range(t0, t0 + chunk)
        rows = [jnp.concatenate([src[part * SLAB + t] for t in ts], axis=1)
                for src in (ft, bt) for part in range(2)]
        y = jnp.dot(e2_ref[...], jnp.concatenate(rows, axis=0), preferred_element_type=F32).astype(BF16)
        for k, t in enumerate(ts):
            h_ref[0, t] = y[:, k * c:(k + 1) * c]


def _filter_spectrum(y1, e2, *, na, nb):
    c = HYENA_WIDTH
    y5 = y1.reshape(2 * HYENA_ORDER, nb, 2, na // 2, c)
    spec = lambda d: pl.BlockSpec((1, nb, 2, SLAB, c), lambda o, k: (2 * o + d, 0, 0, k, 0))
    return pl.pallas_call(
        functools.partial(_s2f_kernel, nb=nb),
        grid=(HYENA_ORDER, na // 2 // SLAB),
        in_specs=[spec(0), spec(1), pl.BlockSpec((2 * nb, 4 * nb), lambda o, k: (0, 0))],
        out_specs=pl.BlockSpec((1, SLAB, 2 * nb, c), lambda o, k: (o, k, 0, 0)),
        out_shape=jax.ShapeDtypeStruct((HYENA_ORDER, na // 2, 2 * nb, c), BF16),
        compiler_params=_cparams("parallel", "arbitrary"),
        name=f"filter_spec_n{na * nb}",
    )(y5, y5, e2)


def _s2_kernel(y_ref, h_ref, e2_ref, e2i_ref, out_ref, ot_ref, *, nb):
    c = HYENA_WIDTH
    yt = jnp.swapaxes(y_ref[0].reshape(nb, 2 * SLAB, -1), 0, 1)
    chunk = SLAB if nb <= 64 else SLAB // 2
    for t0 in range(0, SLAB, chunk):
        ts = range(t0, t0 + chunk)
        d = jnp.concatenate([jnp.concatenate([yt[part * SLAB + t] for t in ts], axis=1)
                             for part in range(2)], axis=0)
        y = jnp.dot(e2_ref[...], d, preferred_element_type=F32).astype(BF16)
        h = jnp.concatenate([h_ref[0, t] for t in ts], axis=1)
        yr, yi, hr, hi = y[:nb], y[nb:], h[:nb], h[nb:]
        p = jnp.concatenate([yr * hr - yi * hi, yr * hi + yi * hr], axis=0)
        o = jnp.dot(e2i_ref[...], p, preferred_element_type=F32).astype(BF16)
        for k, t in enumerate(ts):
            ot_ref[t] = o[:nb, k * c:(k + 1) * c]
            ot_ref[SLAB + t] = o[nb:, k * c:(k + 1) * c]
    out_ref[0] = jnp.swapaxes(ot_ref[...], 0, 1).reshape(nb, 2, SLAB, -1)


def _stage2(y1, hspec, order, e2, e2i, *, na, nb):
    bx = y1.shape[0]
    c = HYENA_WIDTH
    y5 = y1.reshape(bx, nb, 2, na // 2, c)
    yspec = pl.BlockSpec((1, nb, 2, SLAB, c), lambda k, bi: (bi, 0, 0, k, 0))
    mat = pl.BlockSpec((2 * nb, 2 * nb), lambda k, bi: (0, 0))
    out = pl.pallas_call(
        functools.partial(_s2_kernel, nb=nb),
        grid=(na // 2 // SLAB, bx),
        in_specs=[yspec, pl.BlockSpec((1, SLAB, 2 * nb, c), lambda k, bi: (order, k, 0, 0)), mat, mat],
        out_specs=yspec,
        out_shape=jax.ShapeDtypeStruct((bx, nb, 2, na // 2, c), BF16),
        scratch_shapes=[pltpu.VMEM((2 * SLAB, nb, c), BF16)],
        compiler_params=_cparams("parallel", "arbitrary"),
        name=f"dft2_n{na * nb}",
    )(y5, hspec, e2, e2i)
    return out.reshape(bx, nb, na, c)


def _s1inv_chain_kernel(y_ref, g_ref, z_ref, gate_ref, d_ref, m_ref, znew_ref, y1_ref):
    zt = jnp.swapaxes(z_ref[0], 0, 1)
    gt = jnp.swapaxes(gate_ref[0], 0, 1)
    for j in range(zt.shape[0]):
        conv = jnp.dot(g_ref[j], y_ref[0, j], preferred_element_type=F32)
        zn = (gt[j].astype(F32) * (conv + d_ref[...] * zt[j].astype(F32))).astype(BF16)
        znew_ref[0, j] = zn
        y1_ref[0, j] = jnp.dot(m_ref[j], zn, preferred_element_type=F32).astype(BF16)


def _s1inv_last_kernel(y_ref, g_ref, z_ref, gate_ref, d_ref, znew_ref, zt_ref):
    gt = jnp.swapaxes(gate_ref[0], 0, 1)
    for j in range(gt.shape[0]):
        conv = jnp.dot(g_ref[j], y_ref[0, j], preferred_element_type=F32)
        zt_ref[j] = (gt[j].astype(F32) * (conv + d_ref[...] * z_ref[0, j].astype(F32))).astype(BF16)
    znew_ref[0] = jnp.swapaxes(zt_ref[...], 0, 1)


def _stage1_inverse(y3, ginv, z, gate, d, mats, *, na, nb):
    bx = y3.shape[0]
    c = HYENA_WIDTH
    half = na // 2
    l = half * nb
    rj = _rows_per_step(half, nb)
    nat = pl.BlockSpec((1, half, rj, c), lambda j, bi: (bi, 0, j, 0))
    tr = lambda rows: pl.BlockSpec((1, rj, rows, c), lambda j, bi: (bi, j, 0, 0))
    gspec = pl.BlockSpec((rj, half, na), lambda j, bi: (j, 0, 0))
    dspec = pl.BlockSpec((1, c), lambda j, bi: (0, 0))
    gate4 = gate.reshape(bx, half, nb, c)
    if mats is not None:
        return pl.pallas_call(
            _s1inv_chain_kernel,
            grid=(nb // rj, bx),
            in_specs=[tr(na), gspec, nat, nat, dspec, pl.BlockSpec((rj, na, half), lambda j, bi: (j, 0, 0))],
            out_specs=[tr(half), tr(na)],
            out_shape=[jax.ShapeDtypeStruct((bx, nb, half, c), BF16),
                       jax.ShapeDtypeStruct((bx, nb, na, c), BF16)],
            compiler_params=_cparams("parallel", "arbitrary"),
            name=f"idft1_chain_l{l}",
        )(y3, ginv, z.reshape(bx, half, nb, c), gate4, d, mats)
    out = pl.pallas_call(
        _s1inv_last_kernel,
        grid=(nb // rj, bx),
        in_specs=[tr(na), gspec, tr(half), nat, dspec],
        out_specs=nat,
        out_shape=jax.ShapeDtypeStruct((bx, half, nb, c), BF16),
        scratch_shapes=[pltpu.VMEM((rj, half, c), BF16)],
        compiler_params=_cparams("parallel", "arbitrary"),
        name=f"idft1_last_l{l}",
    )(y3, ginv, z, gate4, d)
    return out.reshape(bx, l, c)


def _hyena(z0, x1, x2, filter_params, hyena_d):
    bx, l, c = z0.shape
    na, nb = _fft_plan(l)
    m_fwd, ginv, e2, e2i, e2fb = (jnp.asarray(t).astype(BF16) for t in _dft_tables(l))
    hspec = _filter_spectrum(_filter_stage1(l, *filter_params, m_fwd, na=na, nb=nb), e2fb, na=na, nb=nb)
    y1 = _stage1(z0, m_fwd, na=na, nb=nb)
    y3 = _stage2(y1, hspec, 0, e2, e2i, na=na, nb=nb)
    z1, y1 = _stage1_inverse(y3, ginv, z0, x1, hyena_d[0:1], m_fwd, na=na, nb=nb)
    y3 = _stage2(y1, hspec, 1, e2, e2i, na=na, nb=nb)
    return _stage1_inverse(y3, ginv, z1, x2, hyena_d[1:2], None, na=na, nb=nb)


def _final_kernel(attn_ref, ga_ref, z_ref, gh_ref, x_ref, ag_ref, hg_ref, wo_ref, lg_ref, lb_ref, y_ref):
    def rms(v, g):
        return v * lax.rsqrt(jnp.mean(v * v, axis=-1, keepdims=True) + RMS_EPS) * g

    ma = (rms(attn_ref[0].astype(F32), ag_ref[...]) * ga_ref[0].astype(F32)).astype(BF16)
    mh = (rms(z_ref[0].astype(F32), hg_ref[...]) * gh_ref[0].astype(F32)).astype(BF16)
    a = ATTN_WIDTH
    out = (jnp.dot(ma, wo_ref[0:a, :], preferred_element_type=F32)
           + jnp.dot(mh, wo_ref[a:, :], preferred_element_type=F32))
    h = ((2.0 * DEPTH) ** 0.25) * x_ref[0] + out
    mu = jnp.mean(h, axis=-1, keepdims=True)
    hc = h - mu
    var = jnp.mean(hc * hc, axis=-1, keepdims=True)
    y_ref[0] = hc * lax.rsqrt(var + LN_EPS) * lg_ref[...] + lb_ref[...]


def _final(attn, qkvg, z, gh, x, attn_g, hyena_g, w_out_bf, ln_g, ln_b, *, tt):
    b, l, d = x.shape
    a = ATTN_WIDTH
    tok = lambda w, col=0: pl.BlockSpec((1, tt, w), lambda bi, i: (bi, i, col))
    vec = lambda w: pl.BlockSpec((1, w), lambda bi, i: (0, 0))
    return pl.pallas_call(
        _final_kernel,
        grid=(b, l // tt),
        in_specs=[tok(a), tok(a, 3), tok(a), tok(a), tok(d),
                  vec(a), vec(a), pl.BlockSpec((d, d), lambda bi, i: (0, 0)), vec(d), vec(d)],
        out_specs=tok(d),
        out_shape=jax.ShapeDtypeStruct((b, l, d), F32),
        compiler_params=_cparams("parallel", "arbitrary"),
        name="outproj_ln",
    )(attn, qkvg, z, gh, x, attn_g[None, :], hyena_g[None, :], w_out_bf, ln_g[None, :], ln_b[None, :])


def _layer(x, w_in_bf, conv_w, conv_b, filt_w1, filt_b1, filt_w2, filt_b2, filt_w3, filt_b3,
           filt_freq, filt_w4, hyena_d, attn_norm_g, hyena_norm_g, w_out_bf, ln_g, ln_b):
    b, l, _ = x.shape
    tm = min(1024, l)
    qkvg, z0, x1, x2, gh = _inproj(x, w_in_bf, conv_w, conv_b[None, :], tm=tm)
    attn = _attention(qkvg)
    z = _hyena(z0, x1, x2, (filt_w1, filt_b1, filt_w2, filt_b2, filt_w3, filt_b3, filt_freq, filt_w4), hyena_d)
    return _final(attn, qkvg, z, gh, x, attn_norm_g, hyena_norm_g, w_out_bf, ln_g, ln_b, tt=tm)


def kernel(x_prompt, x_sample, w_in, conv_w, conv_b, filt_w1, filt_b1, filt_w2, filt_b2, filt_w3,
           filt_b3, filt_freq, filt_w4, hyena_d, attn_norm_g, hyena_norm_g, w_out, ln_g, ln_b):
    def trunk(x):
        for i in range(DEPTH):
            x = _layer(x, w_in[i].astype(BF16), conv_w[i], conv_b[i], filt_w1[i], filt_b1[i], filt_w2[i],
                       filt_b2[i], filt_w3[i], filt_b3[i], filt_freq[i], filt_w4[i], hyena_d[i],
                       attn_norm_g[i], hyena_norm_g[i], w_out[i].astype(BF16), ln_g[i], ln_b[i])
        return x
    return (trunk(x_prompt), trunk(x_sample))
```

```python
import functools
import math

import numpy as np
import jax
import jax.numpy as jnp
from jax import lax
from jax.experimental import pallas as pl
from jax.experimental.pallas import tpu as pltpu

F32 = jnp.float32
BF16 = jnp.bfloat16

D_MODEL = 1024
ATTN_WIDTH = 512
HYENA_WIDTH = 512
HEAD_DIM = 64
N_HEADS = 8
HALF_WINDOW = 64
HYENA_ORDER = 2
FILTER_BANDS = 16
FILTER_HIDDEN = 64
FAST_DECAY_PCT = 0.3
SLOW_DECAY_PCT = 1.5
DECAY_TARGET = 1e-2
PROJ_WIDTH = 4096
LN_EPS = 1e-5
RMS_EPS = 1e-6
NEG_INF = -1e30
DEPTH = 1

Q_BLOCK = 128
KEY_WINDOW = 256
SLAB = 16
HEAD_GROUP = 4
GROUP_WIDTH = HEAD_GROUP * HEAD_DIM
COLUMN_CHUNK = 256
LOG2_E = math.log2(math.e)
Q_SCALE = HEAD_DIM ** -0.5 * LOG2_E
VMEM_LIMIT = 56 * 1024 * 1024
FILTER_TILE = 512
STEP_INPUT_BYTES = 2 * 1024 * 1024


def _cparams(*sem):
    return pltpu.CompilerParams(dimension_semantics=sem, vmem_limit_bytes=VMEM_LIMIT)


def _silu(g):
    return g / (1.0 + jnp.exp(-g))


def _inproj_kernel(xp_ref, x_ref, xn_ref, w_ref, cw_ref, cb_ref,
                   qkvg_ref, z0_ref, x1_ref, x2_ref, gh_ref, *, tm, n_tiles):
    i = pl.program_id(1)
    a, c = ATTN_WIDTH, HYENA_WIDTH
    xb = x_ref[0].astype(BF16)

    def proj(lo, hi, lhs=xb):
        return jnp.dot(lhs, w_ref[:, lo:hi], preferred_element_type=F32)

    halo = jnp.concatenate([xp_ref[0, 0], xn_ref[0, 0]], axis=0).astype(BF16)
    xext = jnp.concatenate([xb, halo], axis=0)
    cw2 = COLUMN_CHUNK
    row = lax.broadcasted_iota(jnp.int32, (tm, cw2), 0)
    has_prev, has_next = (i > 0).astype(F32), (i < n_tiles - 1).astype(F32)
    for part, dst in enumerate((z0_ref, x1_ref, x2_ref)):
        for c0 in range(0, c, cw2):
            lo = 4 * a + part * c + c0
            pext = proj(lo, lo + cw2, xext)
            p = pext[:tm]
            below = jnp.where(row == 0, pext[tm + 7:tm + 8] * has_prev, pltpu.roll(p, 1, 0))
            above = jnp.where(row == tm - 1, pext[tm + 8:tm + 9] * has_next, pltpu.roll(p, tm - 1, 0))
            cs = slice(part * c + c0, part * c + c0 + cw2)
            dst[0, :, c0:c0 + cw2] = (cb_ref[:, cs] + below * cw_ref[0:1, cs] + p * cw_ref[1:2, cs]
                                      + above * cw_ref[2:3, cs]).astype(BF16)

    for c0 in range(0, a, cw2):
        qkvg_ref[0, :, 3 * a + c0:3 * a + c0 + cw2] = _silu(proj(3 * a + c0, 3 * a + c0 + cw2)).astype(BF16)
        gh_ref[0, :, c0:c0 + cw2] = _silu(proj(4 * a + 3 * c + c0, 4 * a + 3 * c + c0 + cw2)).astype(BF16)
        qkvg_ref[0, :, c0:c0 + cw2] = (proj(c0, c0 + cw2) * Q_SCALE).astype(BF16)
    for c0 in range(a, 3 * a, cw2):
        qkvg_ref[0, :, c0:c0 + cw2] = proj(c0, c0 + cw2).astype(BF16)


def _inproj(x, w_bf, conv_w, conv_b, *, tm):
    b, l, d = x.shape
    n_tiles = l // tm
    x8 = x.reshape(b, l // 8, 8, d)
    r = tm // 8
    n8 = l // 8
    bf_out = lambda w: jax.ShapeDtypeStruct((b, l, w), BF16)
    row_spec = lambda w: pl.BlockSpec((1, tm, w), lambda bi, i: (bi, i, 0))
    return pl.pallas_call(
        functools.partial(_inproj_kernel, tm=tm, n_tiles=n_tiles),
        grid=(b, n_tiles),
        in_specs=[
            pl.BlockSpec((1, 1, 8, d), lambda bi, i: (bi, jnp.maximum(i * r - 1, 0), 0, 0)),
            pl.BlockSpec((1, tm, d), lambda bi, i: (bi, i, 0)),
            pl.BlockSpec((1, 1, 8, d), lambda bi, i: (bi, jnp.minimum((i + 1) * r, n8 - 1), 0, 0)),
            pl.BlockSpec((d, PROJ_WIDTH), lambda bi, i: (0, 0)),
            pl.BlockSpec((3, 3 * HYENA_WIDTH), lambda bi, i: (0, 0)),
            pl.BlockSpec((1, 3 * HYENA_WIDTH), lambda bi, i: (0, 0)),
        ],
        out_specs=[row_spec(4 * ATTN_WIDTH), row_spec(HYENA_WIDTH), row_spec(HYENA_WIDTH),
                   row_spec(HYENA_WIDTH), row_spec(HYENA_WIDTH)],
        out_shape=[bf_out(4 * ATTN_WIDTH), bf_out(HYENA_WIDTH), bf_out(HYENA_WIDTH),
                   bf_out(HYENA_WIDTH), bf_out(HYENA_WIDTH)],
        compiler_params=_cparams("parallel", "arbitrary"),
        name="inproj",
    )(x8, x, x8, w_bf, conv_w, conv_b)


def _band_bias_tables():
    qi = np.arange(Q_BLOCK)[:, None]
    kj = np.arange(KEY_WINDOW)[None, :]
    rel_band = kj - HALF_WINDOW - qi
    rel_d4 = 4 * ((kj % 64) - 16 - (qi % 32)) + (kj // 64 - qi // 32)
    slopes = np.asarray([2.0 ** (-8.0 * (i + 1) / N_HEADS) for i in range(N_HEADS)], np.float32)
    out = []
    for rel, dil in ((rel_band, 1), (rel_d4, 4), (rel_band, 16)):
        rel = np.abs(rel)
        bias = -(LOG2_E * slopes[:, None, None]) * (rel * dil).astype(np.float32)[None]
        out.append(np.where(rel[None] <= HALF_WINDOW, bias, NEG_INF).astype(np.float32))
    return np.stack(out)


def _attn_kernel(q_ref, kp_ref, k_ref, kn_ref, vp_ref, v_ref, vn_ref, bias_ref, o_ref,
                 qt_ref, kt_ref, vt_ref, acc_ref, m_ref, l_ref, on_ref, ln_ref,
                 *, ti, s16, seq_len):
    i0 = pl.program_id(1) * ti
    hw = HALF_WINDOW
    dn_t = (((1,), (1,)), ((), ()))

    qt_ref[...] = jnp.swapaxes(q_ref[0], 0, 1)
    kt_ref[...] = jnp.swapaxes(jnp.concatenate([kp_ref[0], k_ref[0], kn_ref[0]], axis=0), 0, 1)
    vt_ref[...] = jnp.swapaxes(jnp.concatenate([vp_ref[0], v_ref[0], vn_ref[0]], axis=0), 0, 1)

    col = lax.broadcasted_iota(jnp.int32, (1, KEY_WINDOW), 1)

    pair = 2 * HEAD_DIM
    first = lax.broadcasted_iota(jnp.int32, (1, pair), 1) < HEAD_DIM
    ones_cols = jnp.ones((KEY_WINDOW, pair), BF16)

    def tile(q, kw, vw, pat, kvalid):
        ms, ls, accs = [], [], []
        for pr in range(HEAD_GROUP // 2):
            ps = slice(pr * pair, (pr + 1) * pair)
            qp, kp, vext = q[:, ps], kw[:, ps], jnp.concatenate([vw[:, ps], ones_cols], axis=1)
            halves = []
            for hh in range(2):
                own = first if hh == 0 else jnp.logical_not(first)
                s = lax.dot_general(jnp.where(own, qp, jnp.zeros_like(qp)), kp, dn_t,
                                    preferred_element_type=F32)
                s = s + bias_ref[pat, 2 * pr + hh]
                if kvalid is not None:
                    s = jnp.where(kvalid, s, NEG_INF)
                m = jnp.max(s, axis=-1, keepdims=True)
                pv = jnp.dot(jnp.exp2(s - m).astype(BF16), vext, preferred_element_type=F32)
                halves.append((m, pv))
            (m0, pv0), (m1, pv1) = halves
            ms.append(jnp.where(first, m0, m1))
            accs.append(jnp.where(first, pv0[:, :pair], pv1[:, :pair]))
            ls.append(jnp.where(first, pv0[:, pair:], pv1[:, pair:]))
        cat = lambda parts: jnp.concatenate(parts, axis=1)
        return cat(ms), cat(ls), cat(accs)

    for r in range(SLAB):
        for js in range(0, ti, Q_BLOCK):
            i16 = i0 + js - hw + col
            m, l, acc = tile(qt_ref[r, js:js + Q_BLOCK, :], kt_ref[r, js:js + KEY_WINDOW, :],
                             vt_ref[r, js:js + KEY_WINDOW, :], 2, (i16 >= 0) & (i16 < s16))
            m_ref[r, js:js + Q_BLOCK, :] = m
            l_ref[r, js:js + Q_BLOCK, :] = l
            acc_ref[r, js:js + Q_BLOCK, :] = acc

    for r4 in range(4):
        for is_ in range(0, ti, 32):
            ks = is_ + hw - 16
            q = jnp.concatenate([qt_ref[r4 + 4 * q4, is_:is_ + 32, :] for q4 in range(4)], axis=0)
            kw = jnp.concatenate([kt_ref[r4 + 4 * q4, ks:ks + 64, :] for q4 in range(4)], axis=0)
            vw = jnp.concatenate([vt_ref[r4 + 4 * q4, ks:ks + 64, :] for q4 in range(4)], axis=0)
            i4 = i0 + is_ - 16 + (col % 64)
            inside = is_ - 16 >= 0 and is_ + 48 <= ti
            m4, l4, a4 = tile(q, kw, vw, 1, None if inside else (i4 >= 0) & (i4 < s16))
            for q4 in range(4):
                rows = slice(q4 * 32, (q4 + 1) * 32)
                at = (r4 + 4 * q4, slice(is_, is_ + 32), slice(None))
                m_old, l_old, a_old = m_ref[at], l_ref[at], acc_ref[at]
                m_new = jnp.maximum(m_old, m4[rows])
                alpha, beta = jnp.exp2(m_old - m_new), jnp.exp2(m4[rows] - m_new)
                m_ref[at] = m_new
                l_ref[at] = l_old * alpha + l4[rows] * beta
                acc_ref[at] = a_old * alpha + a4[rows] * beta

    on_ref[...] = jnp.swapaxes((acc_ref[...] / l_ref[...]).astype(BF16), 0, 1)
    ln_ref[...] = jnp.swapaxes(m_ref[...] + jnp.log2(l_ref[...]), 0, 1)

    def window(prev_ref, ref, next_ref, s8):
        lo, hi = s8 - 4, s8 + 12
        parts = []
        if lo < 0:
            parts.append(prev_ref[0, hw + lo:hw])
        parts.append(ref[0, max(lo, 0):min(hi, ti)])
        if hi > ti:
            parts.append(next_ref[0, 0:hi - ti])
        w = parts[0] if len(parts) == 1 else jnp.concatenate(parts, axis=0)
        return w.reshape(KEY_WINDOW, GROUP_WIDTH)

    for s8 in range(0, ti, 8):
        q = q_ref[0, s8:s8 + 8].reshape(Q_BLOCK, GROUP_WIDTH)
        kw = window(kp_ref, k_ref, kn_ref, s8)
        vw = window(vp_ref, v_ref, vn_ref, s8)
        tok = SLAB * (i0 + s8 - 4) + col
        inside = s8 - 4 >= 0 and s8 + 12 <= ti
        m1, l1, a1 = tile(q, kw, vw, 0, None if inside else (tok >= 0) & (tok < seq_len))
        lse_p = ln_ref[s8:s8 + 8].reshape(Q_BLOCK, GROUP_WIDTH)
        o_p = on_ref[s8:s8 + 8].reshape(Q_BLOCK, GROUP_WIDTH).astype(F32)
        m = jnp.maximum(m1, lse_p)
        alpha, beta = jnp.exp2(m1 - m), jnp.exp2(lse_p - m)
        out = (a1 * alpha + o_p * beta) / (l1 * alpha + beta)
        o_ref[0, s8:s8 + 8] = out.astype(BF16).reshape(8, SLAB, GROUP_WIDTH)


def _attention(qkvg):
    b, l, w4 = qkvg.shape
    s16 = l // SLAB
    ti = min(128, s16)
    hw = HALF_WINDOW
    gw = GROUP_WIDTH
    view = qkvg.reshape(b, s16, SLAB, w4)
    ng = ATTN_WIDTH // gw
    n_halo = s16 // hw
    rq = ti // hw

    def main(part):
        return pl.BlockSpec((1, ti, SLAB, gw), lambda bi, i, g: (bi, i, 0, part * ng + g))

    def prev(part):
        return pl.BlockSpec((1, hw, SLAB, gw),
                            lambda bi, i, g: (bi, jnp.maximum(i * rq - 1, 0), 0, part * ng + g))

    def nxt(part):
        return pl.BlockSpec((1, hw, SLAB, gw),
                            lambda bi, i, g: (bi, jnp.minimum((i + 1) * rq, n_halo - 1), 0, part * ng + g))

    vm = lambda shape, dt: pltpu.VMEM(shape, dt)
    out = pl.pallas_call(
        functools.partial(_attn_kernel, ti=ti, s16=s16, seq_len=l),
        grid=(b, s16 // ti, ng),
        in_specs=[main(0), prev(1), main(1), nxt(1), prev(2), main(2), nxt(2),
                  pl.BlockSpec((3, HEAD_GROUP, Q_BLOCK, KEY_WINDOW), lambda bi, i, g: (0, g, 0, 0))],
        out_specs=pl.BlockSpec((1, ti, SLAB, gw), lambda bi, i, g: (bi, i, 0, g)),
        out_shape=jax.ShapeDtypeStruct((b, s16, SLAB, ATTN_WIDTH), BF16),
        scratch_shapes=[vm((SLAB, ti, gw), BF16), vm((SLAB, ti + 2 * hw, gw), BF16),
                        vm((SLAB, ti + 2 * hw, gw), BF16),
                        vm((SLAB, ti, gw), F32), vm((SLAB, ti, gw), F32), vm((SLAB, ti, gw), F32),
                        vm((ti, SLAB, gw), BF16), vm((ti, SLAB, gw), F32)],
        compiler_params=_cparams("parallel", "parallel", "arbitrary"),
        name="band_attn",
    )(view, view, view, view, view, view, view, jnp.asarray(_band_bias_tables()))
    return out.reshape(b, l, ATTN_WIDTH)


def _filter_kernel(w1t_ref, b1_ref, w2t_ref, b2_ref, w3t_ref, b3_ref, fq_ref, w4_ref, fr_ref, dl_ref,
                   m_ref, out_ref, *, rj, half, nb, l):
    hp = lax.Precision.HIGHEST
    c = HYENA_WIDTH
    nbd = FILTER_BANDS
    tn = rj * half
    j0 = pl.program_id(0) * rj

    def lags(shape, axis):
        k = lax.broadcasted_iota(jnp.int32, shape, axis)
        return nb * (k % half) + j0 + k // half

    pos = lags((1, tn), 1).astype(F32)
    t = pos * (1.0 / (l - 1))
    ang = fr_ref[...] * ((2.0 * math.pi / l) * pos)
    dot = lambda a, b: jnp.dot(a, b, precision=hp, preferred_element_type=F32)
    h = (w1t_ref[:, 0:1] * t + dot(w1t_ref[:, 1:1 + nbd], jnp.cos(ang))
         - dot(w1t_ref[:, 1 + nbd:1 + 2 * nbd], jnp.sin(ang)))
    h = jnp.sin(fq_ref[:, 0:1] * (h + b1_ref[...]))
    h = jnp.sin(fq_ref[:, 1:2] * (dot(w2t_ref[...], h) + b2_ref[...]))
    h = jnp.sin(fq_ref[:, 2:3] * (dot(w3t_ref[...], h) + b3_ref[...]))
    ht = h.T.astype(BF16)
    n_col = lags((tn, 1), 0)
    decay = jnp.exp(-(n_col.astype(F32) * (1.0 / (l - 1))) * dl_ref[...])
    decay_b = jnp.where(n_col == 0, 0.0, decay)
    for od in range(2 * HYENA_ORDER):
        f = jnp.dot(ht, w4_ref[:, od * c:(od + 1) * c].astype(BF16), preferred_element_type=F32)
        f = (f * (decay_b if od % 2 else decay)).astype(BF16)
        for j in range(rj):
            out_ref[od, j] = jnp.dot(m_ref[j], f[j * half:(j + 1) * half],
                                     preferred_element_type=F32).astype(BF16)


def _filter_stage1(l, w1, b1, w2, b2, w3, b3, freq, w4, mats, *, na, nb):
    half = na // 2
    rj = max(1, FILTER_TILE // half)
    bands = FILTER_BANDS
    fr = jnp.asarray(np.linspace(1e-4, bands - 1, bands, dtype=np.float32)[:, None])
    max_decay = math.log(DECAY_TARGET) / FAST_DECAY_PCT
    min_decay = math.log(DECAY_TARGET) / SLOW_DECAY_PCT
    deltas = jnp.asarray(np.abs(np.linspace(min_decay, max_decay, HYENA_WIDTH, dtype=np.float32))[None, :])
    full = lambda arr: pl.BlockSpec(arr.shape, lambda i: (0,) * arr.ndim)
    args = (w1.T, b1[:, None], w2.T, b2[:, None], w3.T, b3[:, None], freq.T, w4, fr, deltas)
    return pl.pallas_call(
        functools.partial(_filter_kernel, rj=rj, half=half, nb=nb, l=l),
        grid=(nb // rj,),
        in_specs=[full(a) for a in args] + [pl.BlockSpec((rj, na, half), lambda i: (i, 0, 0))],
        out_specs=pl.BlockSpec((2 * HYENA_ORDER, rj, na, HYENA_WIDTH), lambda i: (0, i, 0, 0)),
        out_shape=jax.ShapeDtypeStruct((2 * HYENA_ORDER, nb, na, HYENA_WIDTH), BF16),
        compiler_params=_cparams("parallel"),
        name=f"filters_dft1_l{l}",
    )(*args, mats)


def _fft_plan(l):
    n = 2 * l
    na = 256 if n >= 32768 else (128 if n >= 8192 else 64)
    return na, n // na


@functools.lru_cache(maxsize=None)
def _dft_tables(l):
    na, nb = _fft_plan(l)
    n = na * nb
    ka = np.arange(na // 2, dtype=np.int64)
    nas = np.arange(na // 2, dtype=np.int64)
    nbs = np.arange(nb, dtype=np.int64)
    num = ((2 * ka[None, :, None] + 1) * (nb * nas[None, None, :] + nbs[:, None, None])) % (2 * n)
    theta = num.astype(np.float64) * (math.pi / n)
    fwd = np.concatenate([np.cos(theta), -np.sin(theta)], axis=1)
    thetat = np.swapaxes(theta, 1, 2)
    inv = (2.0 / n) * np.concatenate([np.cos(thetat), -np.sin(thetat)], axis=2)
    phi = ((nbs[:, None] * nbs[None, :]) % nb).astype(np.float64) * (2.0 * math.pi / nb)
    cc, ss = np.cos(phi), np.sin(phi)
    e2 = np.block([[cc, ss], [-ss, cc]])
    e2i = np.block([[cc, -ss], [ss, cc]])
    e2fb = np.concatenate([e2, np.block([[cc, ss], [ss, -cc]])], axis=1)
    cast = lambda arr: np.ascontiguousarray(arr.astype(np.float32))
    return cast(fwd), cast(inv), cast(e2), cast(e2i), cast(e2fb)


def _rows_per_step(half, nb):
    want = max(1, STEP_INPUT_BYTES // (half * SLAB * HYENA_WIDTH * 2))
    return min(nb, SLAB * want)


def _s1_kernel(src_ref, m_ref, out_ref):
    xt = jnp.swapaxes(src_ref[0], 0, 1)
    for j in range(xt.shape[0]):
        out_ref[0, j] = jnp.dot(m_ref[j], xt[j], preferred_element_type=F32).astype(BF16)


def _stage1(src, mats, *, na, nb):
    bx, l, c = src.shape
    half = na // 2
    rj = _rows_per_step(half, nb)
    return pl.pallas_call(
        _s1_kernel,
        grid=(nb // rj, bx),
        in_specs=[pl.BlockSpec((1, half, rj, c), lambda j, bi: (bi, 0, j, 0)),
                  pl.BlockSpec((rj, na, half), lambda j, bi: (j, 0, 0))],
        out_specs=pl.BlockSpec((1, rj, na, c), lambda j, bi: (bi, j, 0, 0)),
        out_shape=jax.ShapeDtypeStruct((bx, nb, na, c), BF16),
        compiler_params=_cparams("parallel", "arbitrary"),
        name=f"dft1_l{l}",
    )(src.reshape(bx, half, nb, c), mats)


def _s2f_kernel(yf_ref, yb_ref, e2_ref, h_ref, *, nb):
    c = HYENA_WIDTH
    ft = jnp.swapaxes(yf_ref[0].reshape(nb, 2 * SLAB, -1), 0, 1)
    bt = jnp.swapaxes(yb_ref[0].reshape(nb, 2 * SLAB, -1), 0, 1)
    chunk = SLAB if nb <= 64 else SLAB // 2
    for t0 in range(0, SLAB, chunk):
        ts = range(t0, t0 + chunk)
        rows = [jnp.concatenate([src[part * SLAB + t] for t in ts], axis=1)
                for src in (ft, bt) for part in range(2)]
        y = jnp.dot(e2_ref[...], jnp.concatenate(rows, axis=0), preferred_element_type=F32).astype(BF16)
        for k, t in enumerate(ts):
            h_ref[0, t] = y[:, k * c:(k + 1) * c]


def _filter_spectrum(y1, e2, *, na, nb):
    c = HYENA_WIDTH
    y5 = y1.reshape(2 * HYENA_ORDER, nb, 2, na // 2, c)
    spec = lambda d: pl.BlockSpec((1, nb, 2, SLAB, c), lambda o, k: (2 * o + d, 0, 0, k, 0))
    return pl.pallas_call(
        functools.partial(_s2f_kernel, nb=nb),
        grid=(HYENA_ORDER, na // 2 // SLAB),
        in_specs=[spec(0), spec(1), pl.BlockSpec((2 * nb, 4 * nb), lambda o, k: (0, 0))],
        out_specs=pl.BlockSpec((1, SLAB, 2 * nb, c), lambda o, k: (o, k, 0, 0)),
        out_shape=jax.ShapeDtypeStruct((HYENA_ORDER, na // 2, 2 * nb, c), BF16),
        compiler_params=_cparams("parallel", "arbitrary"),
        name=f"filter_spec_n{na * nb}",
    )(y5, y5, e2)


def _s2_kernel(y_ref, h_ref, e2_ref, e2i_ref, out_ref, ot_ref, *, nb):
    c = HYENA_WIDTH
    yt = jnp.swapaxes(y_ref[0].reshape(nb, 2 * SLAB, -1), 0, 1)
    chunk = SLAB if nb <= 64 else SLAB // 2
    for t0 in range(0, SLAB, chunk):
        ts = range(t0, t0 + chunk)
        d = jnp.concatenate([jnp.concatenate([yt[part * SLAB + t] for t in ts], axis=1)
                             for part in range(2)], axis=0)
        y = jnp.dot(e2_ref[...], d, preferred_element_type=F32).astype(BF16)
        h = jnp.concatenate([h_ref[0, t] for t in ts], axis=1)
        yr, yi, hr, hi = y[:nb], y[nb:], h[:nb], h[nb:]
        p = jnp.concatenate([yr * hr - yi * hi, yr * hi + yi * hr], axis=0)
        o = jnp.dot(e2i_ref[...], p, preferred_element_type=F32).astype(BF16)
        for k, t in enumerate(ts):
            ot_ref[t] = o[:nb, k * c:(k + 1) * c]
            ot_ref[SLAB + t] = o[nb:, k * c:(k + 1) * c]
    out_ref[0] = jnp.swapaxes(ot_ref[...], 0, 1).reshape(nb, 2, SLAB, -1)


def _stage2(y1, hspec, order, e2, e2i, *, na, nb):
    bx = y1.shape[0]
    c = HYENA_WIDTH
    y5 = y1.reshape(bx, nb, 2, na // 2, c)
    yspec = pl.BlockSpec((1, nb, 2, SLAB, c), lambda k, bi: (bi, 0, 0, k, 0))
    mat = pl.BlockSpec((2 * nb, 2 * nb), lambda k, bi: (0, 0))
    out = pl.pallas_call(
        functools.partial(_s2_kernel, nb=nb),
        grid=(na // 2 // SLAB, bx),
        in_specs=[yspec, pl.BlockSpec((1, SLAB, 2 * nb, c), lambda k, bi: (order, k, 0, 0)), mat, mat],
        out_specs=yspec,
        out_shape=jax.ShapeDtypeStruct((bx, nb, 2, na // 2, c), BF16),
        scratch_shapes=[pltpu.VMEM((2 * SLAB, nb, c), BF16)],
        compiler_params=_cparams("parallel", "arbitrary"),
        name=f"dft2_n{na * nb}",
    )(y5, hspec, e2, e2i)
    return out.reshape(bx, nb, na, c)


def _s1inv_chain_kernel(y_ref, g_ref, z_ref, gate_ref, d_ref, m_ref, znew_ref, y1_ref, conv_ref):
    zt = jnp.swapaxes(z_ref[0], 0, 1)
    gt = jnp.swapaxes(gate_ref[0], 0, 1)
    rows = zt.shape[0]
    for j in range(rows):
        conv_ref[j] = jnp.dot(g_ref[j], y_ref[0, j], preferred_element_type=F32)
    for j in range(rows):
        znew_ref[0, j] = (gt[j].astype(F32) * (conv_ref[j] + d_ref[...] * zt[j].astype(F32))).astype(BF16)
    for j in range(rows):
        y1_ref[0, j] = jnp.dot(m_ref[j], znew_ref[0, j], preferred_element_type=F32).astype(BF16)


def _s1inv_last_kernel(y_ref, g_ref, z_ref, gate_ref, d_ref, znew_ref, zt_ref):
    gt = jnp.swapaxes(gate_ref[0], 0, 1)
    for j in range(gt.shape[0]):
        conv = jnp.dot(g_ref[j], y_ref[0, j], preferred_element_type=F32)
        zt_ref[j] = (gt[j].astype(F32) * (conv + d_ref[...] * z_ref[0, j].astype(F32))).astype(BF16)
    znew_ref[0] = jnp.swapaxes(zt_ref[...], 0, 1)


def _stage1_inverse(y3, ginv, z, gate, d, mats, *, na, nb):
    bx = y3.shape[0]
    c = HYENA_WIDTH
    half = na // 2
    l = half * nb
    rj = _rows_per_step(half, nb)
    nat = pl.BlockSpec((1, half, rj, c), lambda j, bi: (bi, 0, j, 0))
    tr = lambda rows: pl.BlockSpec((1, rj, rows, c), lambda j, bi: (bi, j, 0, 0))
    gspec = pl.BlockSpec((rj, half, na), lambda j, bi: (j, 0, 0))
    dspec = pl.BlockSpec((1, c), lambda j, bi: (0, 0))
    gate4 = gate.reshape(bx, half, nb, c)
    if mats is not None:
        return pl.pallas_call(
            _s1inv_chain_kernel,
            grid=(nb // rj, bx),
            in_specs=[tr(na), gspec, nat, nat, dspec, pl.BlockSpec((rj, na, half), lambda j, bi: (j, 0, 0))],
            out_specs=[tr(half), tr(na)],
            out_shape=[jax.ShapeDtypeStruct((bx, nb, half, c), BF16),
                       jax.ShapeDtypeStruct((bx, nb, na, c), BF16)],
            scratch_shapes=[pltpu.VMEM((rj, half, c), F32)],
            compiler_params=_cparams("parallel", "arbitrary"),
            name=f"idft1_chain_l{l}",
        )(y3, ginv, z.reshape(bx, half, nb, c), gate4, d, mats)
    out = pl.pallas_call(
        _s1inv_last_kernel,
        grid=(nb // rj, bx),
        in_specs=[tr(na), gspec, tr(half), nat, dspec],
        out_specs=nat,
        out_shape=jax.ShapeDtypeStruct((bx, half, nb, c), BF16),
        scratch_shapes=[pltpu.VMEM((rj, half, c), BF16)],
        compiler_params=_cparams("parallel", "arbitrary"),
        name=f"idft1_last_l{l}",
    )(y3, ginv, z, gate4, d)
    return out.reshape(bx, l, c)


def _hyena(z0, x1, x2, filter_params, hyena_d):
    bx, l, c = z0.shape
    na, nb = _fft_plan(l)
    m_fwd, ginv, e2, e2i, e2fb = (jnp.asarray(t).astype(BF16) for t in _dft_tables(l))
    hspec = _filter_spectrum(_filter_stage1(l, *filter_params, m_fwd, na=na, nb=nb), e2fb, na=na, nb=nb)
    y1 = _stage1(z0, m_fwd, na=na, nb=nb)
    y3 = _stage2(y1, hspec, 0, e2, e2i, na=na, nb=nb)
    z1, y1 = _stage1_inverse(y3, ginv, z0, x1, hyena_d[0:1], m_fwd, na=na, nb=nb)
    y3 = _stage2(y1, hspec, 1, e2, e2i, na=na, nb=nb)
    return _stage1_inverse(y3, ginv, z1, x2, hyena_d[1:2], None, na=na, nb=nb)


def _final_kernel(attn_ref, ga_ref, z_ref, gh_ref, x_ref, ag_ref, hg_ref, wo_ref, lg_ref, lb_ref, y_ref):
    def rms(v, g):
        return v * lax.rsqrt(jnp.mean(v * v, axis=-1, keepdims=True) + RMS_EPS) * g

    ma = (rms(attn_ref[0].astype(F32), ag_ref[...]) * ga_ref[0].astype(F32)).astype(BF16)
    mh = (rms(z_ref[0].astype(F32), hg_ref[...]) * gh_ref[0].astype(F32)).astype(BF16)
    a = ATTN_WIDTH
    out = (jnp.dot(ma, wo_ref[0:a, :], preferred_element_type=F32)
           + jnp.dot(mh, wo_ref[a:, :], preferred_element_type=F32))
    h = ((2.0 * DEPTH) ** 0.25) * x_ref[0] + out
    mu = jnp.mean(h, axis=-1, keepdims=True)
    hc = h - mu
    var = jnp.mean(hc * hc, axis=-1, keepdims=True)
    y_ref[0] = hc * lax.rsqrt(var + LN_EPS) * lg_ref[...] + lb_ref[...]


def _final(attn, qkvg, z, gh, x, attn_g, hyena_g, w_out_bf, ln_g, ln_b, *, tt):
    b, l, d = x.shape
    a = ATTN_WIDTH
    tok = lambda w, col=0: pl.BlockSpec((1, tt, w), lambda bi, i: (bi, i, col))
    vec = lambda w: pl.BlockSpec((1, w), lambda bi, i: (0, 0))
    return pl.pallas_call(
        _final_kernel,
        grid=(b, l // tt),
        in_specs=[tok(a), tok(a, 3), tok(a), tok(a), tok(d),
                  vec(a), vec(a), pl.BlockSpec((d, d), lambda bi, i: (0, 0)), vec(d), vec(d)],
        out_specs=tok(d),
        out_shape=jax.ShapeDtypeStruct((b, l, d), F32),
        compiler_params=_cparams("parallel", "arbitrary"),
        name="outproj_ln",
    )(attn, qkvg, z, gh, x, attn_g[None, :], hyena_g[None, :], w_out_bf, ln_g[None, :], ln_b[None, :])


def _layer(x, w_in_bf, conv_w, conv_b, filt_w1, filt_b1, filt_w2, filt_b2, filt_w3, filt_b3,
           filt_freq, filt_w4, hyena_d, attn_norm_g, hyena_norm_g, w_out_bf, ln_g, ln_b):
    b, l, _ = x.shape
    tm = min(1024, l)
    qkvg, z0, x1, x2, gh = _inproj(x, w_in_bf, conv_w, conv_b[None, :], tm=tm)
    attn = _attention(qkvg)
    z = _hyena(z0, x1, x2, (filt_w1, filt_b1, filt_w2, filt_b2, filt_w3, filt_b3, filt_freq, filt_w4), hyena_d)
    return _final(attn, qkvg, z, gh, x, attn_norm_g, hyena_norm_g, w_out_bf, ln_g, ln_b, tt=tm)


def kernel(x_prompt, x_sample, w_in, conv_w, conv_b, filt_w1, filt_b1, filt_w2, filt_b2, filt_w3,
           filt_b3, filt_freq, filt_w4, hyena_d, attn_norm_g, hyena_norm_g, w_out, ln_g, ln_b):
    def trunk(x):
        for i in range(DEPTH):
            x = _layer(x, w_in[i].astype(BF16), conv_w[i], conv_b[i], filt_w1[i], filt_b1[i], filt_w2[i],
                       filt_b2[i], filt_w3[i], filt_b3[i], filt_freq[i], filt_w4[i], hyena_d[i],
                       attn_norm_g[i], hyena_norm_g[i], w_out[i].astype(BF16), ln_g[i], ln_b[i])
        return x
    return (trunk(x_prompt), trunk(x_sample))
```

```python
import functools
import math

import numpy as np
import jax
import jax.numpy as jnp
from jax import lax
from jax.experimental import pallas as pl
from jax.experimental.pallas import tpu as pltpu

F32 = jnp.float32
BF16 = jnp.bfloat16

D_MODEL = 1024
ATTN_WIDTH = 512
HYENA_WIDTH = 512
HEAD_DIM = 64
N_HEADS = 8
HALF_WINDOW = 64
HYENA_ORDER = 2
FILTER_BANDS = 16
FILTER_HIDDEN = 64
FAST_DECAY_PCT = 0.3
SLOW_DECAY_PCT = 1.5
DECAY_TARGET = 1e-2
PROJ_WIDTH = 4096
LN_EPS = 1e-5
RMS_EPS = 1e-6
NEG_INF = -1e30
DEPTH = 1

Q_BLOCK = 128
KEY_WINDOW = 256
SLAB = 16
HEAD_GROUP = 4
GROUP_WIDTH = HEAD_GROUP * HEAD_DIM
COLUMN_CHUNK = 256
LOG2_E = math.log2(math.e)
Q_SCALE = HEAD_DIM ** -0.5 * LOG2_E
VMEM_LIMIT = 56 * 1024 * 1024
FILTER_TILE = 1024
STEP_INPUT_BYTES = 2 * 1024 * 1024


def _cparams(*sem):
    return pltpu.CompilerParams(dimension_semantics=sem, vmem_limit_bytes=VMEM_LIMIT)


def _silu(g):
    return g / (1.0 + jnp.exp(-g))


def _inproj_kernel(xp_ref, x_ref, xn_ref, w_ref, cw_ref, cb_ref,
                   qkvg_ref, z0_ref, x1_ref, x2_ref, gh_ref, *, tm, n_tiles):
    i = pl.program_id(1)
    a, c = ATTN_WIDTH, HYENA_WIDTH
    xb = x_ref[0].astype(BF16)

    def proj(lo, hi, lhs=xb):
        return jnp.dot(lhs, w_ref[:, lo:hi], preferred_element_type=F32)

    halo = jnp.concatenate([xp_ref[0, 0], xn_ref[0, 0]], axis=0).astype(BF16)
    xext = jnp.concatenate([xb, halo], axis=0)
    cw2 = COLUMN_CHUNK
    row = lax.broadcasted_iota(jnp.int32, (tm, cw2), 0)
    has_prev, has_next = (i > 0).astype(F32), (i < n_tiles - 1).astype(F32)
    for part, dst in enumerate((z0_ref, x1_ref, x2_ref)):
        for c0 in range(0, c, cw2):
            lo = 4 * a + part * c + c0
            pext = proj(lo, lo + cw2, xext)
            p = pext[:tm]
            below = jnp.where(row == 0, pext[tm + 7:tm + 8] * has_prev, pltpu.roll(p, 1, 0))
            above = jnp.where(row == tm - 1, pext[tm + 8:tm + 9] * has_next, pltpu.roll(p, tm - 1, 0))
            cs = slice(part * c + c0, part * c + c0 + cw2)
            dst[0, :, c0:c0 + cw2] = (cb_ref[:, cs] + below * cw_ref[0:1, cs] + p * cw_ref[1:2, cs]
                                      + above * cw_ref[2:3, cs]).astype(BF16)

    for c0 in range(0, a, cw2):
        qkvg_ref[0, :, 3 * a + c0:3 * a + c0 + cw2] = _silu(proj(3 * a + c0, 3 * a + c0 + cw2)).astype(BF16)
        gh_ref[0, :, c0:c0 + cw2] = _silu(proj(4 * a + 3 * c + c0, 4 * a + 3 * c + c0 + cw2)).astype(BF16)
        qkvg_ref[0, :, c0:c0 + cw2] = (proj(c0, c0 + cw2) * Q_SCALE).astype(BF16)
    for c0 in range(a, 3 * a, cw2):
        qkvg_ref[0, :, c0:c0 + cw2] = proj(c0, c0 + cw2).astype(BF16)


def _inproj(x, w_bf, conv_w, conv_b, *, tm):
    b, l, d = x.shape
    n_tiles = l // tm
    x8 = x.reshape(b, l // 8, 8, d)
    r = tm // 8
    n8 = l // 8
    bf_out = lambda w: jax.ShapeDtypeStruct((b, l, w), BF16)
    row_spec = lambda w: pl.BlockSpec((1, tm, w), lambda bi, i: (bi, i, 0))
    return pl.pallas_call(
        functools.partial(_inproj_kernel, tm=tm, n_tiles=n_tiles),
        grid=(b, n_tiles),
        in_specs=[
            pl.BlockSpec((1, 1, 8, d), lambda bi, i: (bi, jnp.maximum(i * r - 1, 0), 0, 0)),
            pl.BlockSpec((1, tm, d), lambda bi, i: (bi, i, 0)),
            pl.BlockSpec((1, 1, 8, d), lambda bi, i: (bi, jnp.minimum((i + 1) * r, n8 - 1), 0, 0)),
            pl.BlockSpec((d, PROJ_WIDTH), lambda bi, i: (0, 0)),
            pl.BlockSpec((3, 3 * HYENA_WIDTH), lambda bi, i: (0, 0)),
            pl.BlockSpec((1, 3 * HYENA_WIDTH), lambda bi, i: (0, 0)),
        ],
        out_specs=[row_spec(4 * ATTN_WIDTH), row_spec(HYENA_WIDTH), row_spec(HYENA_WIDTH),
                   row_spec(HYENA_WIDTH), row_spec(HYENA_WIDTH)],
        out_shape=[bf_out(4 * ATTN_WIDTH), bf_out(HYENA_WIDTH), bf_out(HYENA_WIDTH),
                   bf_out(HYENA_WIDTH), bf_out(HYENA_WIDTH)],
        compiler_params=_cparams("parallel", "arbitrary"),
        name="inproj",
    )(x8, x, x8, w_bf, conv_w, conv_b)


def _band_bias_tables():
    qi = np.arange(Q_BLOCK)[:, None]
    kj = np.arange(KEY_WINDOW)[None, :]
    rel_band = kj - HALF_WINDOW - qi
    rel_d4 = 4 * ((kj % 64) - 16 - (qi % 32)) + (kj // 64 - qi // 32)
    slopes = np.asarray([2.0 ** (-8.0 * (i + 1) / N_HEADS) for i in range(N_HEADS)], np.float32)
    out = []
    for rel, dil in ((rel_band, 1), (rel_d4, 4), (rel_band, 16)):
        rel = np.abs(rel)
        bias = -(LOG2_E * slopes[:, None, None]) * (rel * dil).astype(np.float32)[None]
        out.append(np.where(rel[None] <= HALF_WINDOW, bias, NEG_INF).astype(np.float32))
    return np.stack(out)


def _attn_kernel(q_ref, kp_ref, k_ref, kn_ref, vp_ref, v_ref, vn_ref, bias_ref, o_ref,
                 qt_ref, kt_ref, vt_ref, acc_ref, m_ref, l_ref, on_ref, ln_ref,
                 *, ti, s16, seq_len):
    i0 = pl.program_id(1) * ti
    hw = HALF_WINDOW
    dn_t = (((1,), (1,)), ((), ()))

    qt_ref[...] = jnp.swapaxes(q_ref[0], 0, 1)
    kt_ref[...] = jnp.swapaxes(jnp.concatenate([kp_ref[0], k_ref[0], kn_ref[0]], axis=0), 0, 1)
    vt_ref[...] = jnp.swapaxes(jnp.concatenate([vp_ref[0], v_ref[0], vn_ref[0]], axis=0), 0, 1)

    col = lax.broadcasted_iota(jnp.int32, (1, KEY_WINDOW), 1)

    pair = 2 * HEAD_DIM
    first = lax.broadcasted_iota(jnp.int32, (1, pair), 1) < HEAD_DIM
    ones_cols = jnp.ones((KEY_WINDOW, pair), BF16)

    def tile(q, kw, vw, pat, kvalid):
        ms, ls, accs = [], [], []
        for pr in range(HEAD_GROUP // 2):
            ps = slice(pr * pair, (pr + 1) * pair)
            qp, kp, vext = q[:, ps], kw[:, ps], jnp.concatenate([vw[:, ps], ones_cols], axis=1)
            halves = []
            for hh in range(2):
                own = first if hh == 0 else jnp.logical_not(first)
                s = lax.dot_general(jnp.where(own, qp, jnp.zeros_like(qp)), kp, dn_t,
                                    preferred_element_type=F32)
                s = s + bias_ref[pat, 2 * pr + hh]
                if kvalid is not None:
                    s = jnp.where(kvalid, s, NEG_INF)
                m = jnp.max(s, axis=-1, keepdims=True)
                pv = jnp.dot(jnp.exp2(s - m).astype(BF16), vext, preferred_element_type=F32)
                halves.append((m, pv))
            (m0, pv0), (m1, pv1) = halves
            ms.append(jnp.where(first, m0, m1))
            accs.append(jnp.where(first, pv0[:, :pair], pv1[:, :pair]))
            ls.append(jnp.where(first, pv0[:, pair:], pv1[:, pair:]))
        cat = lambda parts: jnp.concatenate(parts, axis=1)
        return cat(ms), cat(ls), cat(accs)

    for r in range(SLAB):
        for js in range(0, ti, Q_BLOCK):
            i16 = i0 + js - hw + col
            m, l, acc = tile(qt_ref[r, js:js + Q_BLOCK, :], kt_ref[r, js:js + KEY_WINDOW, :],
                             vt_ref[r, js:js + KEY_WINDOW, :], 2, (i16 >= 0) & (i16 < s16))
            m_ref[r, js:js + Q_BLOCK, :] = m
            l_ref[r, js:js + Q_BLOCK, :] = l
            acc_ref[r, js:js + Q_BLOCK, :] = acc

    for r4 in range(4):
        for is_ in range(0, ti, 32):
            ks = is_ + hw - 16
            q = jnp.concatenate([qt_ref[r4 + 4 * q4, is_:is_ + 32, :] for q4 in range(4)], axis=0)
            kw = jnp.concatenate([kt_ref[r4 + 4 * q4, ks:ks + 64, :] for q4 in range(4)], axis=0)
            vw = jnp.concatenate([vt_ref[r4 + 4 * q4, ks:ks + 64, :] for q4 in range(4)], axis=0)
            i4 = i0 + is_ - 16 + (col % 64)
            inside = is_ - 16 >= 0 and is_ + 48 <= ti
            m4, l4, a4 = tile(q, kw, vw, 1, None if inside else (i4 >= 0) & (i4 < s16))
            for q4 in range(4):
                rows = slice(q4 * 32, (q4 + 1) * 32)
                at = (r4 + 4 * q4, slice(is_, is_ + 32), slice(None))
                m_old, l_old, a_old = m_ref[at], l_ref[at], acc_ref[at]
                m_new = jnp.maximum(m_old, m4[rows])
                alpha, beta = jnp.exp2(m_old - m_new), jnp.exp2(m4[rows] - m_new)
                m_ref[at] = m_new
                l_ref[at] = l_old * alpha + l4[rows] * beta
                acc_ref[at] = a_old * alpha + a4[rows] * beta

    on_ref[...] = jnp.swapaxes((acc_ref[...] / l_ref[...]).astype(BF16), 0, 1)
    ln_ref[...] = jnp.swapaxes(m_ref[...] + jnp.log2(l_ref[...]), 0, 1)

    def window(prev_ref, ref, next_ref, s8):
        lo, hi = s8 - 4, s8 + 12
        parts = []
        if lo < 0:
            parts.append(prev_ref[0, hw + lo:hw])
        parts.append(ref[0, max(lo, 0):min(hi, ti)])
        if hi > ti:
            parts.append(next_ref[0, 0:hi - ti])
        w = parts[0] if len(parts) == 1 else jnp.concatenate(parts, axis=0)
        return w.reshape(KEY_WINDOW, GROUP_WIDTH)

    for s8 in range(0, ti, 8):
        q = q_ref[0, s8:s8 + 8].reshape(Q_BLOCK, GROUP_WIDTH)
        kw = window(kp_ref, k_ref, kn_ref, s8)
        vw = window(vp_ref, v_ref, vn_ref, s8)
        tok = SLAB * (i0 + s8 - 4) + col
        inside = s8 - 4 >= 0 and s8 + 12 <= ti
        m1, l1, a1 = tile(q, kw, vw, 0, None if inside else (tok >= 0) & (tok < seq_len))
        lse_p = ln_ref[s8:s8 + 8].reshape(Q_BLOCK, GROUP_WIDTH)
        o_p = on_ref[s8:s8 + 8].reshape(Q_BLOCK, GROUP_WIDTH).astype(F32)
        m = jnp.maximum(m1, lse_p)
        alpha, beta = jnp.exp2(m1 - m), jnp.exp2(lse_p - m)
        out = (a1 * alpha + o_p * beta) / (l1 * alpha + beta)
        o_ref[0, s8:s8 + 8] = out.astype(BF16).reshape(8, SLAB, GROUP_WIDTH)


def _attention(qkvg):
    b, l, w4 = qkvg.shape
    s16 = l // SLAB
    ti = min(128, s16)
    hw = HALF_WINDOW
    gw = GROUP_WIDTH
    view = qkvg.reshape(b, s16, SLAB, w4)
    ng = ATTN_WIDTH // gw
    n_halo = s16 // hw
    rq = ti // hw

    def main(part):
        return pl.BlockSpec((1, ti, SLAB, gw), lambda bi, i, g: (bi, i, 0, part * ng + g))

    def prev(part):
        return pl.BlockSpec((1, hw, SLAB, gw),
                            lambda bi, i, g: (bi, jnp.maximum(i * rq - 1, 0), 0, part * ng + g))

    def nxt(part):
        return pl.BlockSpec((1, hw, SLAB, gw),
                            lambda bi, i, g: (bi, jnp.minimum((i + 1) * rq, n_halo - 1), 0, part * ng + g))

    vm = lambda shape, dt: pltpu.VMEM(shape, dt)
    out = pl.pallas_call(
        functools.partial(_attn_kernel, ti=ti, s16=s16, seq_len=l),
        grid=(b, s16 // ti, ng),
        in_specs=[main(0), prev(1), main(1), nxt(1), prev(2), main(2), nxt(2),
                  pl.BlockSpec((3, HEAD_GROUP, Q_BLOCK, KEY_WINDOW), lambda bi, i, g: (0, g, 0, 0))],
        out_specs=pl.BlockSpec((1, ti, SLAB, gw), lambda bi, i, g: (bi, i, 0, g)),
        out_shape=jax.ShapeDtypeStruct((b, s16, SLAB, ATTN_WIDTH), BF16),
        scratch_shapes=[vm((SLAB, ti, gw), BF16), vm((SLAB, ti + 2 * hw, gw), BF16),
                        vm((SLAB, ti + 2 * hw, gw), BF16),
                        vm((SLAB, ti, gw), F32), vm((SLAB, ti, gw), F32), vm((SLAB, ti, gw), F32),
                        vm((ti, SLAB, gw), BF16), vm((ti, SLAB, gw), F32)],
        compiler_params=_cparams("parallel", "parallel", "arbitrary"),
        name="band_attn",
    )(view, view, view, view, view, view, view, jnp.asarray(_band_bias_tables()))
    return out.reshape(b, l, ATTN_WIDTH)


def _filter_kernel(w1t_ref, b1_ref, w2t_ref, b2_ref, w3t_ref, b3_ref, fq_ref, w4_ref, fr_ref, dl_ref,
                   m_ref, out_ref, *, rj, half, nb, l):
    hp = lax.Precision.HIGHEST
    c = HYENA_WIDTH
    nbd = FILTER_BANDS
    tn = rj * half
    j0 = pl.program_id(0) * rj

    def lags(shape, axis):
        k = lax.broadcasted_iota(jnp.int32, shape, axis)
        return nb * (k % half) + j0 + k // half

    pos = lags((1, tn), 1).astype(F32)
    t = pos * (1.0 / (l - 1))
    ang = fr_ref[...] * ((2.0 * math.pi / l) * pos)
    dot = lambda a, b: jnp.dot(a, b, precision=hp, preferred_element_type=F32)
    h = (w1t_ref[:, 0:1] * t + dot(w1t_ref[:, 1:1 + nbd], jnp.cos(ang))
         - dot(w1t_ref[:, 1 + nbd:1 + 2 * nbd], jnp.sin(ang)))
    h = jnp.sin(fq_ref[:, 0:1] * (h + b1_ref[...]))
    h = jnp.sin(fq_ref[:, 1:2] * (dot(w2t_ref[...], h) + b2_ref[...]))
    h = jnp.sin(fq_ref[:, 2:3] * (dot(w3t_ref[...], h) + b3_ref[...]))
    ht = h.T.astype(BF16)
    n_col = lags((tn, 1), 0)
    decay = jnp.exp(-(n_col.astype(F32) * (1.0 / (l - 1))) * dl_ref[...])
    decay_b = jnp.where(n_col == 0, 0.0, decay)
    for od in range(2 * HYENA_ORDER):
        f = jnp.dot(ht, w4_ref[:, od * c:(od + 1) * c].astype(BF16), preferred_element_type=F32)
        f = (f * (decay_b if od % 2 else decay)).astype(BF16)
        for j in range(rj):
            out_ref[od, j] = jnp.dot(m_ref[j], f[j * half:(j + 1) * half],
                                     preferred_element_type=F32).astype(BF16)


def _filter_stage1(l, w1, b1, w2, b2, w3, b3, freq, w4, mats, *, na, nb):
    half = na // 2
    rj = max(1, FILTER_TILE // half)
    bands = FILTER_BANDS
    fr = jnp.asarray(np.linspace(1e-4, bands - 1, bands, dtype=np.float32)[:, None])
    max_decay = math.log(DECAY_TARGET) / FAST_DECAY_PCT
    min_decay = math.log(DECAY_TARGET) / SLOW_DECAY_PCT
    deltas = jnp.asarray(np.abs(np.linspace(min_decay, max_decay, HYENA_WIDTH, dtype=np.float32))[None, :])
    full = lambda arr: pl.BlockSpec(arr.shape, lambda i: (0,) * arr.ndim)
    args = (w1.T, b1[:, None], w2.T, b2[:, None], w3.T, b3[:, None], freq.T, w4, fr, deltas)
    return pl.pallas_call(
        functools.partial(_filter_kernel, rj=rj, half=half, nb=nb, l=l),
        grid=(nb // rj,),
        in_specs=[full(a) for a in args] + [pl.BlockSpec((rj, na, half), lambda i: (i, 0, 0))],
        out_specs=pl.BlockSpec((2 * HYENA_ORDER, rj, na, HYENA_WIDTH), lambda i: (0, i, 0, 0)),
        out_shape=jax.ShapeDtypeStruct((2 * HYENA_ORDER, nb, na, HYENA_WIDTH), BF16),
        compiler_params=_cparams("parallel"),
        name=f"filters_dft1_l{l}",
    )(*args, mats)


def _fft_plan(l):
    n = 2 * l
    na = 256 if n >= 32768 else (128 if n >= 8192 else 64)
    return na, n // na


@functools.lru_cache(maxsize=None)
def _dft_tables(l):
    na, nb = _fft_plan(l)
    n = na * nb
    ka = np.arange(na // 2, dtype=np.int64)
    nas = np.arange(na // 2, dtype=np.int64)
    nbs = np.arange(nb, dtype=np.int64)
    num = ((2 * ka[None, :, None] + 1) * (nb * nas[None, None, :] + nbs[:, None, None])) % (2 * n)
    theta = num.astype(np.float64) * (math.pi / n)
    fwd = np.concatenate([np.cos(theta), -np.sin(theta)], axis=1)
    thetat = np.swapaxes(theta, 1, 2)
    inv = (2.0 / n) * np.concatenate([np.cos(thetat), -np.sin(thetat)], axis=2)
    phi = ((nbs[:, None] * nbs[None, :]) % nb).astype(np.float64) * (2.0 * math.pi / nb)
    cc, ss = np.cos(phi), np.sin(phi)
    e2 = np.block([[cc, ss], [-ss, cc]])
    e2i = np.block([[cc, -ss], [ss, cc]])
    e2fb = np.concatenate([e2, np.block([[cc, ss], [ss, -cc]])], axis=1)
    cast = lambda arr: np.ascontiguousarray(arr.astype(np.float32))
    return cast(fwd), cast(inv), cast(e2), cast(e2i), cast(e2fb)


def _rows_per_step(half, nb):
    want = max(1, STEP_INPUT_BYTES // (half * SLAB * HYENA_WIDTH * 2))
    return min(nb, SLAB * want)


def _s1_kernel(src_ref, m_ref, out_ref):
    xt = jnp.swapaxes(src_ref[0], 0, 1)
    for j in range(xt.shape[0]):
        out_ref[0, j] = jnp.dot(m_ref[j], xt[j], preferred_element_type=F32).astype(BF16)


def _stage1(src, mats, *, na, nb):
    bx, l, c = src.shape
    half = na // 2
    rj = _rows_per_step(half, nb)
    return pl.pallas_call(
        _s1_kernel,
        grid=(nb // rj, bx),
        in_specs=[pl.BlockSpec((1, half, rj, c), lambda j, bi: (bi, 0, j, 0)),
                  pl.BlockSpec((rj, na, half), lambda j, bi: (j, 0, 0))],
        out_specs=pl.BlockSpec((1, rj, na, c), lambda j, bi: (bi, j, 0, 0)),
        out_shape=jax.ShapeDtypeStruct((bx, nb, na, c), BF16),
        compiler_params=_cparams("parallel", "arbitrary"),
        name=f"dft1_l{l}",
    )(src.reshape(bx, half, nb, c), mats)


def _s2f_kernel(yf_ref, yb_ref, e2_ref, h_ref, *, nb):
    c = HYENA_WIDTH
    ft = jnp.swapaxes(yf_ref[0].reshape(nb, 2 * SLAB, -1), 0, 1)
    bt = jnp.swapaxes(yb_ref[0].reshape(nb, 2 * SLAB, -1), 0, 1)
    chunk = SLAB if nb <= 64 else SLAB // 2
    for t0 in range(0, SLAB, chunk):
        ts = range(t0, t0 + chunk)
        rows = [jnp.concatenate([src[part * SLAB + t] for t in ts], axis=1)
                for src in (ft, bt) for part in range(2)]
        y = jnp.dot(e2_ref[...], jnp.concatenate(rows, axis=0), preferred_element_type=F32).astype(BF16)
        for k, t in enumerate(ts):
            h_ref[0, t] = y[:, k * c:(k + 1) * c]


def _filter_spectrum(y1, e2, *, na, nb):
    c = HYENA_WIDTH
    y5 = y1.reshape(2 * HYENA_ORDER, nb, 2, na // 2, c)
    spec = lambda d: pl.BlockSpec((1, nb, 2, SLAB, c), lambda o, k: (2 * o + d, 0, 0, k, 0))
    return pl.pallas_call(
        functools.partial(_s2f_kernel, nb=nb),
        grid=(HYENA_ORDER, na // 2 // SLAB),
        in_specs=[spec(0), spec(1), pl.BlockSpec((2 * nb, 4 * nb), lambda o, k: (0, 0))],
        out_specs=pl.BlockSpec((1, SLAB, 2 * nb, c), lambda o, k: (o, k, 0, 0)),
        out_shape=jax.ShapeDtypeStruct((HYENA_ORDER, na // 2, 2 * nb, c), BF16),
        compiler_params=_cparams("parallel", "arbitrary"),
        name=f"filter_spec_n{na * nb}",
    )(y5, y5, e2)


def _s2_kernel(y_ref, h_ref, e2_ref, e2i_ref, out_ref, ot_ref, *, nb):
    c = HYENA_WIDTH
    yt = jnp.swapaxes(y_ref[0].reshape(nb, 2 * SLAB, -1), 0, 1)
    chunk = SLAB if nb <= 64 else SLAB // 2
    for t0 in range(0, SLAB, chunk):
        ts = range(t0, t0 + chunk)
        d = jnp.concatenate([jnp.concatenate([yt[part * SLAB + t] for t in ts], axis=1)
                             for part in range(2)], axis=0)
        y = jnp.dot(e2_ref[...], d, preferred_element_type=F32).astype(BF16)
        h = jnp.concatenate([h_ref[0, t] for t in ts], axis=1)
        yr, yi, hr, hi = y[:nb], y[nb:], h[:nb], h[nb:]
        p = jnp.concatenate([yr * hr - yi * hi, yr * hi + yi * hr], axis=0)
        o = jnp.dot(e2i_ref[...], p, preferred_element_type=F32).astype(BF16)
        for k, t in enumerate(ts):
            ot_ref[t] = o[:nb, k * c:(k + 1) * c]
            ot_ref[SLAB + t] = o[nb:, k * c:(k + 1) * c]
    out_ref[0] = jnp.swapaxes(ot_ref[...], 0, 1).reshape(nb, 2, SLAB, -1)


def _stage2(y1, hspec, order, e2, e2i, *, na, nb):
    bx = y1.shape[0]
    c = HYENA_WIDTH
    y5 = y1.reshape(bx, nb, 2, na // 2, c)
    yspec = pl.BlockSpec((1, nb, 2, SLAB, c), lambda k, bi: (bi, 0, 0, k, 0))
    mat = pl.BlockSpec((2 * nb, 2 * nb), lambda k, bi: (0, 0))
    out = pl.pallas_call(
        functools.partial(_s2_kernel, nb=nb),
        grid=(na // 2 // SLAB, bx),
        in_specs=[yspec, pl.BlockSpec((1, SLAB, 2 * nb, c), lambda k, bi: (order, k, 0, 0)), mat, mat],
        out_specs=yspec,
        out_shape=jax.ShapeDtypeStruct((bx, nb, 2, na // 2, c), BF16),
        scratch_shapes=[pltpu.VMEM((2 * SLAB, nb, c), BF16)],
        compiler_params=_cparams("parallel", "arbitrary"),
        name=f"dft2_n{na * nb}",
    )(y5, hspec, e2, e2i)
    return out.reshape(bx, nb, na, c)


def _s1inv_chain_kernel(y_ref, g_ref, z_ref, gate_ref, d_ref, m_ref, znew_ref, y1_ref, conv_ref):
    zt = jnp.swapaxes(z_ref[0], 0, 1)
    gt = jnp.swapaxes(gate_ref[0], 0, 1)
    rows = zt.shape[0]
    for j in range(rows):
        conv_ref[j] = jnp.dot(g_ref[j], y_ref[0, j], preferred_element_type=F32)
    for j in range(rows):
        znew_ref[0, j] = (gt[j].astype(F32) * (conv_ref[j] + d_ref[...] * zt[j].astype(F32))).astype(BF16)
    for j in range(rows):
        y1_ref[0, j] = jnp.dot(m_ref[j], znew_ref[0, j], preferred_element_type=F32).astype(BF16)


def _s1inv_last_kernel(y_ref, g_ref, z_ref, gate_ref, d_ref, znew_ref, zt_ref):
    gt = jnp.swapaxes(gate_ref[0], 0, 1)
    for j in range(gt.shape[0]):
        conv = jnp.dot(g_ref[j], y_ref[0, j], preferred_element_type=F32)
        zt_ref[j] = (gt[j].astype(F32) * (conv + d_ref[...] * z_ref[0, j].astype(F32))).astype(BF16)
    znew_ref[0] = jnp.swapaxes(zt_ref[...], 0, 1)


def _stage1_inverse(y3, ginv, z, gate, d, mats, *, na, nb):
    bx = y3.shape[0]
    c = HYENA_WIDTH
    half = na // 2
    l = half * nb
    rj = _rows_per_step(half, nb)
    nat = pl.BlockSpec((1, half, rj, c), lambda j, bi: (bi, 0, j, 0))
    tr = lambda rows: pl.BlockSpec((1, rj, rows, c), lambda j, bi: (bi, j, 0, 0))
    gspec = pl.BlockSpec((rj, half, na), lambda j, bi: (j, 0, 0))
    dspec = pl.BlockSpec((1, c), lambda j, bi: (0, 0))
    gate4 = gate.reshape(bx, half, nb, c)
    if mats is not None:
        return pl.pallas_call(
            _s1inv_chain_kernel,
            grid=(nb // rj, bx),
            in_specs=[tr(na), gspec, nat, nat, dspec, pl.BlockSpec((rj, na, half), lambda j, bi: (j, 0, 0))],
            out_specs=[tr(half), tr(na)],
            out_shape=[jax.ShapeDtypeStruct((bx, nb, half, c), BF16),
                       jax.ShapeDtypeStruct((bx, nb, na, c), BF16)],
            scratch_shapes=[pltpu.VMEM((rj, half, c), F32)],
            compiler_params=_cparams("parallel", "arbitrary"),
            name=f"idft1_chain_l{l}",
        )(y3, ginv, z.reshape(bx, half, nb, c), gate4, d, mats)
    out = pl.pallas_call(
        _s1inv_last_kernel,
        grid=(nb // rj, bx),
        in_specs=[tr(na), gspec, tr(half), nat, dspec],
        out_specs=nat,
        out_shape=jax.ShapeDtypeStruct((bx, half, nb, c), BF16),
        scratch_shapes=[pltpu.VMEM((rj, half, c), BF16)],
        compiler_params=_cparams("parallel", "arbitrary"),
        name=f"idft1_last_l{l}",
    )(y3, ginv, z, gate4, d)
    return out.reshape(bx, l, c)


def _hyena(z0, x1, x2, filter_params, hyena_d):
    bx, l, c = z0.shape
    na, nb = _fft_plan(l)
    m_fwd, ginv, e2, e2i, e2fb = (jnp.asarray(t).astype(BF16) for t in _dft_tables(l))
    hspec = _filter_spectrum(_filter_stage1(l, *filter_params, m_fwd, na=na, nb=nb), e2fb, na=na, nb=nb)
    y1 = _stage1(z0, m_fwd, na=na, nb=nb)
    y3 = _stage2(y1, hspec, 0, e2, e2i, na=na, nb=nb)
    z1, y1 = _stage1_inverse(y3, ginv, z0, x1, hyena_d[0:1], m_fwd, na=na, nb=nb)
    y3 = _stage2(y1, hspec, 1, e2, e2i, na=na, nb=nb)
    return _stage1_inverse(y3, ginv, z1, x2, hyena_d[1:2], None, na=na, nb=nb)


def _final_kernel(attn_ref, ga_ref, z_ref, gh_ref, x_ref, ag_ref, hg_ref, wo_ref, lg_ref, lb_ref, y_ref):
    def rms(v, g):
        return v * lax.rsqrt(jnp.mean(v * v, axis=-1, keepdims=True) + RMS_EPS) * g

    ma = (rms(attn_ref[0].astype(F32), ag_ref[...]) * ga_ref[0].astype(F32)).astype(BF16)
    mh = (rms(z_ref[0].astype(F32), hg_ref[...]) * gh_ref[0].astype(F32)).astype(BF16)
    a = ATTN_WIDTH
    out = (jnp.dot(ma, wo_ref[0:a, :], preferred_element_type=F32)
           + jnp.dot(mh, wo_ref[a:, :], preferred_element_type=F32))
    h = ((2.0 * DEPTH) ** 0.25) * x_ref[0] + out
    mu = jnp.mean(h, axis=-1, keepdims=True)
    hc = h - mu
    var = jnp.mean(hc * hc, axis=-1, keepdims=True)
    y_ref[0] = hc * lax.rsqrt(var + LN_EPS) * lg_ref[...] + lb_ref[...]


def _final(attn, qkvg, z, gh, x, attn_g, hyena_g, w_out_bf, ln_g, ln_b, *, tt):
    b, l, d = x.shape
    a = ATTN_WIDTH
    tok = lambda w, col=0: pl.BlockSpec((1, tt, w), lambda bi, i: (bi, i, col))
    vec = lambda w: pl.BlockSpec((1, w), lambda bi, i: (0, 0))
    return pl.pallas_call(
        _final_kernel,
        grid=(b, l // tt),
        in_specs=[tok(a), tok(a, 3), tok(a), tok(a), tok(d),
                  vec(a), vec(a), pl.BlockSpec((d, d), lambda bi, i: (0, 0)), vec(d), vec(d)],
        out_specs=tok(d),
        out_shape=jax.ShapeDtypeStruct((b, l, d), F32),
        compiler_params=_cparams("parallel", "arbitrary"),
        name="outproj_ln",
    )(attn, qkvg, z, gh, x, attn_g[None, :], hyena_g[None, :], w_out_bf, ln_g[None, :], ln_b[None, :])


def _layer(x, w_in_bf, conv_w, conv_b, filt_w1, filt_b1, filt_w2, filt_b2, filt_w3, filt_b3,
           filt_freq, filt_w4, hyena_d, attn_norm_g, hyena_norm_g, w_out_bf, ln_g, ln_b):
    b, l, _ = x.shape
    tm = min(1024, l)
    qkvg, z0, x1, x2, gh = _inproj(x, w_in_bf, conv_w, conv_b[None, :], tm=tm)
    attn = _attention(qkvg)
    z = _hyena(z0, x1, x2, (filt_w1, filt_b1, filt_w2, filt_b2, filt_w3, filt_b3, filt_freq, filt_w4), hyena_d)
    return _final(attn, qkvg, z, gh, x, attn_norm_g, hyena_norm_g, w_out_bf, ln_g, ln_b, tt=tm)


def kernel(x_prompt, x_sample, w_in, conv_w, conv_b, filt_w1, filt_b1, filt_w2, filt_b2, filt_w3,
           filt_b3, filt_freq, filt_w4, hyena_d, attn_norm_g, hyena_norm_g, w_out, ln_g, ln_b):
    def trunk(x):
        for i in range(DEPTH):
            x = _layer(x, w_in[i].astype(BF16), conv_w[i], conv_b[i], filt_w1[i], filt_b1[i], filt_w2[i],
                       filt_b2[i], filt_w3[i], filt_b3[i], filt_freq[i], filt_w4[i], hyena_d[i],
                       attn_norm_g[i], hyena_norm_g[i], w_out[i].astype(BF16), ln_g[i], ln_b[i])
        return x
    return (trunk(x_prompt), trunk(x_sample))
```

```python
import functools
import math

import numpy as np
import jax
import jax.numpy as jnp
from jax import lax
from jax.experimental import pallas as pl
from jax.experimental.pallas import tpu as pltpu

F32 = jnp.float32
BF16 = jnp.bfloat16

D_MODEL = 1024
ATTN_WIDTH = 512
HYENA_WIDTH = 512
HEAD_DIM = 64
N_HEADS = 8
HALF_WINDOW = 64
HYENA_ORDER = 2
FILTER_BANDS = 16
FILTER_HIDDEN = 64
FAST_DECAY_PCT = 0.3
SLOW_DECAY_PCT = 1.5
DECAY_TARGET = 1e-2
PROJ_WIDTH = 4096
LN_EPS = 1e-5
RMS_EPS = 1e-6
NEG_INF = -1e30
DEPTH = 1

Q_BLOCK = 128
KEY_WINDOW = 256
SLAB = 16
HEAD_GROUP = 4
GROUP_WIDTH = HEAD_GROUP * HEAD_DIM
COLUMN_CHUNK = 256
LOG2_E = math.log2(math.e)
Q_SCALE = HEAD_DIM ** -0.5 * LOG2_E
VMEM_LIMIT = 56 * 1024 * 1024
FILTER_TILE = 1024
STEP_INPUT_BYTES = 2 * 1024 * 1024


def _cparams(*sem):
    return pltpu.CompilerParams(dimension_semantics=sem, vmem_limit_bytes=VMEM_LIMIT)


def _silu(g):
    return g / (1.0 + jnp.exp(-g))


def _inproj_kernel(xp_ref, x_ref, xn_ref, w_ref, cw_ref, cb_ref,
                   qkvg_ref, z0_ref, x1_ref, x2_ref, gh_ref, *, tm, n_tiles):
    i = pl.program_id(1)
    a, c = ATTN_WIDTH, HYENA_WIDTH
    xb = x_ref[0].astype(BF16)

    def proj(lo, hi, lhs=xb):
        return jnp.dot(lhs, w_ref[:, lo:hi], preferred_element_type=F32)

    halo = jnp.concatenate([xp_ref[0, 0], xn_ref[0, 0]], axis=0).astype(BF16)
    xext = jnp.concatenate([xb, halo], axis=0)
    cw2 = COLUMN_CHUNK
    row = lax.broadcasted_iota(jnp.int32, (tm, cw2), 0)
    has_prev, has_next = (i > 0).astype(F32), (i < n_tiles - 1).astype(F32)
    for part, dst in enumerate((z0_ref, x1_ref, x2_ref)):
        for c0 in range(0, c, cw2):
            lo = 4 * a + part * c + c0
            pext = proj(lo, lo + cw2, xext)
            p = pext[:tm]
            below = jnp.where(row == 0, pext[tm + 7:tm + 8] * has_prev, pltpu.roll(p, 1, 0))
            above = jnp.where(row == tm - 1, pext[tm + 8:tm + 9] * has_next, pltpu.roll(p, tm - 1, 0))
            cs = slice(part * c + c0, part * c + c0 + cw2)
            dst[0, :, c0:c0 + cw2] = (cb_ref[:, cs] + below * cw_ref[0:1, cs] + p * cw_ref[1:2, cs]
                                      + above * cw_ref[2:3, cs]).astype(BF16)

    for c0 in range(0, a, cw2):
        qkvg_ref[0, :, 3 * a + c0:3 * a + c0 + cw2] = _silu(proj(3 * a + c0, 3 * a + c0 + cw2)).astype(BF16)
        gh_ref[0, :, c0:c0 + cw2] = _silu(proj(4 * a + 3 * c + c0, 4 * a + 3 * c + c0 + cw2)).astype(BF16)
        qkvg_ref[0, :, c0:c0 + cw2] = (proj(c0, c0 + cw2) * Q_SCALE).astype(BF16)
    for c0 in range(a, 3 * a, cw2):
        qkvg_ref[0, :, c0:c0 + cw2] = proj(c0, c0 + cw2).astype(BF16)


def _inproj(x, w_bf, conv_w, conv_b, *, tm):
    b, l, d = x.shape
    n_tiles = l // tm
    x8 = x.reshape(b, l // 8, 8, d)
    r = tm // 8
    n8 = l // 8
    bf_out = lambda w: jax.ShapeDtypeStruct((b, l, w), BF16)
    row_spec = lambda w: pl.BlockSpec((1, tm, w), lambda bi, i: (bi, i, 0))
    return pl.pallas_call(
        functools.partial(_inproj_kernel, tm=tm, n_tiles=n_tiles),
        grid=(b, n_tiles),
        in_specs=[
            pl.BlockSpec((1, 1, 8, d), lambda bi, i: (bi, jnp.maximum(i * r - 1, 0), 0, 0)),
            pl.BlockSpec((1, tm, d), lambda bi, i: (bi, i, 0)),
            pl.BlockSpec((1, 1, 8, d), lambda bi, i: (bi, jnp.minimum((i + 1) * r, n8 - 1), 0, 0)),
            pl.BlockSpec((d, PROJ_WIDTH), lambda bi, i: (0, 0)),
            pl.BlockSpec((3, 3 * HYENA_WIDTH), lambda bi, i: (0, 0)),
            pl.BlockSpec((1, 3 * HYENA_WIDTH), lambda bi, i: (0, 0)),
        ],
        out_specs=[row_spec(4 * ATTN_WIDTH), row_spec(HYENA_WIDTH), row_spec(HYENA_WIDTH),
                   row_spec(HYENA_WIDTH), row_spec(HYENA_WIDTH)],
        out_shape=[bf_out(4 * ATTN_WIDTH), bf_out(HYENA_WIDTH), bf_out(HYENA_WIDTH),
                   bf_out(HYENA_WIDTH), bf_out(HYENA_WIDTH)],
        compiler_params=_cparams("parallel", "arbitrary"),
        name="inproj",
    )(x8, x, x8, w_bf, conv_w, conv_b)


def _band_bias_tables():
    qi = np.arange(Q_BLOCK)[:, None]
    kj = np.arange(KEY_WINDOW)[None, :]
    rel_band = kj - HALF_WINDOW - qi
    rel_d4 = 4 * ((kj % 64) - 16 - (qi % 32)) + (kj // 64 - qi // 32)
    slopes = np.asarray([2.0 ** (-8.0 * (i + 1) / N_HEADS) for i in range(N_HEADS)], np.float32)
    out = []
    for rel, dil in ((rel_band, 1), (rel_d4, 4), (rel_band, 16)):
        rel = np.abs(rel)
        bias = -(LOG2_E * slopes[:, None, None]) * (rel * dil).astype(np.float32)[None]
        out.append(np.where(rel[None] <= HALF_WINDOW, bias, NEG_INF).astype(np.float32))
    return np.stack(out)


def _attn_kernel(q_ref, kp_ref, k_ref, kn_ref, vp_ref, v_ref, vn_ref, bias_ref, o_ref,
                 qt_ref, kt_ref, vt_ref, acc_ref, m_ref, l_ref, on_ref, ln_ref,
                 *, ti, s16, seq_len):
    i0 = pl.program_id(1) * ti
    hw = HALF_WINDOW
    dn_t = (((1,), (1,)), ((), ()))

    qt_ref[...] = jnp.swapaxes(q_ref[0], 0, 1)
    kt_ref[...] = jnp.swapaxes(jnp.concatenate([kp_ref[0], k_ref[0], kn_ref[0]], axis=0), 0, 1)
    vt_ref[...] = jnp.swapaxes(jnp.concatenate([vp_ref[0], v_ref[0], vn_ref[0]], axis=0), 0, 1)

    col = lax.broadcasted_iota(jnp.int32, (1, KEY_WINDOW), 1)

    pair = 2 * HEAD_DIM
    first = lax.broadcasted_iota(jnp.int32, (1, pair), 1) < HEAD_DIM
    ones_cols = jnp.ones((KEY_WINDOW, pair), BF16)

    def tile(q, kw, vw, pat, kvalid):
        ms, ls, accs = [], [], []
        for pr in range(HEAD_GROUP // 2):
            ps = slice(pr * pair, (pr + 1) * pair)
            qp, kp, vext = q[:, ps], kw[:, ps], jnp.concatenate([vw[:, ps], ones_cols], axis=1)
            halves = []
            for hh in range(2):
                own = first if hh == 0 else jnp.logical_not(first)
                s = lax.dot_general(jnp.where(own, qp, jnp.zeros_like(qp)), kp, dn_t,
                                    preferred_element_type=F32)
                s = s + bias_ref[pat, 2 * pr + hh]
                if kvalid is not None:
                    s = jnp.where(kvalid, s, NEG_INF)
                m = jnp.max(s, axis=-1, keepdims=True)
                pv = jnp.dot(jnp.exp2(s - m).astype(BF16), vext, preferred_element_type=F32)
                halves.append((m, pv))
            (m0, pv0), (m1, pv1) = halves
            ms.append(jnp.where(first, m0, m1))
            accs.append(jnp.where(first, pv0[:, :pair], pv1[:, :pair]))
            ls.append(jnp.where(first, pv0[:, pair:], pv1[:, pair:]))
        cat = lambda parts: jnp.concatenate(parts, axis=1)
        return cat(ms), cat(ls), cat(accs)

    for r in range(SLAB):
        for js in range(0, ti, Q_BLOCK):
            i16 = i0 + js - hw + col
            m, l, acc = tile(qt_ref[r, js:js + Q_BLOCK, :], kt_ref[r, js:js + KEY_WINDOW, :],
                             vt_ref[r, js:js + KEY_WINDOW, :], 2, (i16 >= 0) & (i16 < s16))
            m_ref[r, js:js + Q_BLOCK, :] = m
            l_ref[r, js:js + Q_BLOCK, :] = l
            acc_ref[r, js:js + Q_BLOCK, :] = acc

    for r4 in range(4):
        for is_ in range(0, ti, 32):
            ks = is_ + hw - 16
            q = jnp.concatenate([qt_ref[r4 + 4 * q4, is_:is_ + 32, :] for q4 in range(4)], axis=0)
            kw = jnp.concatenate([kt_ref[r4 + 4 * q4, ks:ks + 64, :] for q4 in range(4)], axis=0)
            vw = jnp.concatenate([vt_ref[r4 + 4 * q4, ks:ks + 64, :] for q4 in range(4)], axis=0)
            i4 = i0 + is_ - 16 + (col % 64)
            inside = is_ - 16 >= 0 and is_ + 48 <= ti
            m4, l4, a4 = tile(q, kw, vw, 1, None if inside else (i4 >= 0) & (i4 < s16))
            for q4 in range(4):
                rows = slice(q4 * 32, (q4 + 1) * 32)
                at = (r4 + 4 * q4, slice(is_, is_ + 32), slice(None))
                m_old, l_old, a_old = m_ref[at], l_ref[at], acc_ref[at]
                m_new = jnp.maximum(m_old, m4[rows])
                alpha, beta = jnp.exp2(m_old - m_new), jnp.exp2(m4[rows] - m_new)
                m_ref[at] = m_new
                l_ref[at] = l_old * alpha + l4[rows] * beta
                acc_ref[at] = a_old * alpha + a4[rows] * beta

    on_ref[...] = jnp.swapaxes((acc_ref[...] / l_ref[...]).astype(BF16), 0, 1)
    ln_ref[...] = jnp.swapaxes(m_ref[...] + jnp.log2(l_ref[...]), 0, 1)

    def window(prev_ref, ref, next_ref, s8):
        lo, hi = s8 - 4, s8 + 12
        parts = []
        if lo < 0:
            parts.append(prev_ref[0, hw + lo:hw])
        parts.append(ref[0, max(lo, 0):min(hi, ti)])
        if hi > ti:
            parts.append(next_ref[0, 0:hi - ti])
        w = parts[0] if len(parts) == 1 else jnp.concatenate(parts, axis=0)
        return w.reshape(KEY_WINDOW, GROUP_WIDTH)

    for s8 in range(0, ti, 8):
        q = q_ref[0, s8:s8 + 8].reshape(Q_BLOCK, GROUP_WIDTH)
        kw = window(kp_ref, k_ref, kn_ref, s8)
        vw = window(vp_ref, v_ref, vn_ref, s8)
        tok = SLAB * (i0 + s8 - 4) + col
        inside = s8 - 4 >= 0 and s8 + 12 <= ti
        m1, l1, a1 = tile(q, kw, vw, 0, None if inside else (tok >= 0) & (tok < seq_len))
        lse_p = ln_ref[s8:s8 + 8].reshape(Q_BLOCK, GROUP_WIDTH)
        o_p = on_ref[s8:s8 + 8].reshape(Q_BLOCK, GROUP_WIDTH).astype(F32)
        m = jnp.maximum(m1, lse_p)
        alpha, beta = jnp.exp2(m1 - m), jnp.exp2(lse_p - m)
        out = (a1 * alpha + o_p * beta) / (l1 * alpha + beta)
        o_ref[0, s8:s8 + 8] = out.astype(BF16).reshape(8, SLAB, GROUP_WIDTH)


def _attention(qkvg):
    b, l, w4 = qkvg.shape
    s16 = l // SLAB
    ti = min(128, s16)
    hw = HALF_WINDOW
    gw = GROUP_WIDTH
    view = qkvg.reshape(b, s16, SLAB, w4)
    ng = ATTN_WIDTH // gw
    n_halo = s16 // hw
    rq = ti // hw

    def main(part):
        return pl.BlockSpec((1, ti, SLAB, gw), lambda bi, i, g: (bi, i, 0, part * ng + g))

    def prev(part):
        return pl.BlockSpec((1, hw, SLAB, gw),
                            lambda bi, i, g: (bi, jnp.maximum(i * rq - 1, 0), 0, part * ng + g))

    def nxt(part):
        return pl.BlockSpec((1, hw, SLAB, gw),
                            lambda bi, i, g: (bi, jnp.minimum((i + 1) * rq, n_halo - 1), 0, part * ng + g))

    vm = lambda shape, dt: pltpu.VMEM(shape, dt)
    out = pl.pallas_call(
        functools.partial(_attn_kernel, ti=ti, s16=s16, seq_len=l),
        grid=(b, s16 // ti, ng),
        in_specs=[main(0), prev(1), main(1), nxt(1), prev(2), main(2), nxt(2),
                  pl.BlockSpec((3, HEAD_GROUP, Q_BLOCK, KEY_WINDOW), lambda bi, i, g: (0, g, 0, 0))],
        out_specs=pl.BlockSpec((1, ti, SLAB, gw), lambda bi, i, g: (bi, i, 0, g)),
        out_shape=jax.ShapeDtypeStruct((b, s16, SLAB, ATTN_WIDTH), BF16),
        scratch_shapes=[vm((SLAB, ti, gw), BF16), vm((SLAB, ti + 2 * hw, gw), BF16),
                        vm((SLAB, ti + 2 * hw, gw), BF16),
                        vm((SLAB, ti, gw), F32), vm((SLAB, ti, gw), F32), vm((SLAB, ti, gw), F32),
                        vm((ti, SLAB, gw), BF16), vm((ti, SLAB, gw), F32)],
        compiler_params=_cparams("parallel", "parallel", "arbitrary"),
        name="band_attn",
    )(view, view, view, view, view, view, view, jnp.asarray(_band_bias_tables()))
    return out.reshape(b, l, ATTN_WIDTH)


def _filter_kernel(w1t_ref, b1_ref, w2t_ref, b2_ref, w3t_ref, b3_ref, fq_ref, w4_ref, fr_ref, dl_ref,
                   m_ref, out_ref, *, rj, half, nb, l):
    hp = lax.Precision.HIGHEST
    c = HYENA_WIDTH
    nbd = FILTER_BANDS
    tn = rj * half
    j0 = pl.program_id(0) * rj

    def lags(shape, axis):
        k = lax.broadcasted_iota(jnp.int32, shape, axis)
        return nb * (k % half) + j0 + k // half

    pos = lags((1, tn), 1).astype(F32)
    t = pos * (1.0 / (l - 1))
    ang = fr_ref[...] * ((2.0 * math.pi / l) * pos)
    dot = lambda a, b: jnp.dot(a, b, precision=hp, preferred_element_type=F32)
    h = (w1t_ref[:, 0:1] * t + dot(w1t_ref[:, 1:1 + nbd], jnp.cos(ang))
         - dot(w1t_ref[:, 1 + nbd:1 + 2 * nbd], jnp.sin(ang)))
    h = jnp.sin(fq_ref[:, 0:1] * (h + b1_ref[...]))
    h = jnp.sin(fq_ref[:, 1:2] * (dot(w2t_ref[...], h) + b2_ref[...]))
    h = jnp.sin(fq_ref[:, 2:3] * (dot(w3t_ref[...], h) + b3_ref[...]))
    ht = h.T.astype(BF16)
    n_col = lags((tn, 1), 0)
    decay = jnp.exp(-(n_col.astype(F32) * (1.0 / (l - 1))) * dl_ref[...])
    decay_b = jnp.where(n_col == 0, 0.0, decay)
    for od in range(2 * HYENA_ORDER):
        f = jnp.dot(ht, w4_ref[:, od * c:(od + 1) * c].astype(BF16), preferred_element_type=F32)
        f = (f * (decay_b if od % 2 else decay)).astype(BF16)
        for j in range(rj):
            out_ref[od, j] = jnp.dot(m_ref[j], f[j * half:(j + 1) * half],
                                     preferred_element_type=F32).astype(BF16)


def _filter_stage1(l, w1, b1, w2, b2, w3, b3, freq, w4, mats, *, na, nb):
    half = na // 2
    rj = max(1, FILTER_TILE // half)
    bands = FILTER_BANDS
    fr = jnp.asarray(np.linspace(1e-4, bands - 1, bands, dtype=np.float32)[:, None])
    max_decay = math.log(DECAY_TARGET) / FAST_DECAY_PCT
    min_decay = math.log(DECAY_TARGET) / SLOW_DECAY_PCT
    deltas = jnp.asarray(np.abs(np.linspace(min_decay, max_decay, HYENA_WIDTH, dtype=np.float32))[None, :])
    full = lambda arr: pl.BlockSpec(arr.shape, lambda i: (0,) * arr.ndim)
    args = (w1.T, b1[:, None], w2.T, b2[:, None], w3.T, b3[:, None], freq.T, w4, fr, deltas)
    return pl.pallas_call(
        functools.partial(_filter_kernel, rj=rj, half=half, nb=nb, l=l),
        grid=(nb // rj,),
        in_specs=[full(a) for a in args] + [pl.BlockSpec((rj, na, half), lambda i: (i, 0, 0))],
        out_specs=pl.BlockSpec((2 * HYENA_ORDER, rj, na, HYENA_WIDTH), lambda i: (0, i, 0, 0)),
        out_shape=jax.ShapeDtypeStruct((2 * HYENA_ORDER, nb, na, HYENA_WIDTH), BF16),
        compiler_params=_cparams("parallel"),
        name=f"filters_dft1_l{l}",
    )(*args, mats)


def _fft_plan(l):
    n = 2 * l
    na = 256 if n >= 32768 else (128 if n >= 8192 else 64)
    return na, n // na


@functools.lru_cache(maxsize=None)
def _dft_tables(l):
    na, nb = _fft_plan(l)
    n = na * nb
    ka = np.arange(na // 2, dtype=np.int64)
    nas = np.arange(na // 2, dtype=np.int64)
    nbs = np.arange(nb, dtype=np.int64)
    num = ((2 * ka[None, :, None] + 1) * (nb * nas[None, None, :] + nbs[:, None, None])) % (2 * n)
    theta = num.astype(np.float64) * (math.pi / n)
    fwd = np.concatenate([np.cos(theta), -np.sin(theta)], axis=1)
    thetat = np.swapaxes(theta, 1, 2)
    inv = (2.0 / n) * np.concatenate([np.cos(thetat), -np.sin(thetat)], axis=2)
    phi = ((nbs[:, None] * nbs[None, :]) % nb).astype(np.float64) * (2.0 * math.pi / nb)
    cc, ss = np.cos(phi), np.sin(phi)
    e2 = np.block([[cc, ss], [-ss, cc]])
    e2i = np.block([[cc, -ss], [ss, cc]])
    e2fb = np.concatenate([e2, np.block([[cc, ss], [ss, -cc]])], axis=1)
    cast = lambda arr: np.ascontiguousarray(arr.astype(np.float32))
    return cast(fwd), cast(inv), cast(e2), cast(e2i), cast(e2fb)


def _rows_per_step(half, nb):
    want = max(1, STEP_INPUT_BYTES // (half * SLAB * HYENA_WIDTH * 2))
    return min(nb, SLAB * want)


def _s1_kernel(src_ref, m_ref, out_ref):
    xt = jnp.swapaxes(src_ref[0], 0, 1)
    for j in range(xt.shape[0]):
        out_ref[0, j] = jnp.dot(m_ref[j], xt[j], preferred_element_type=F32).astype(BF16)


def _stage1(src, mats, *, na, nb):
    bx, l, c = src.shape
    half = na // 2
    rj = _rows_per_step(half, nb)
    return pl.pallas_call(
        _s1_kernel,
        grid=(nb // rj, bx),
        in_specs=[pl.BlockSpec((1, half, rj, c), lambda j, bi: (bi, 0, j, 0)),
                  pl.BlockSpec((rj, na, half), lambda j, bi: (j, 0, 0))],
        out_specs=pl.BlockSpec((1, rj, na, c), lambda j, bi: (bi, j, 0, 0)),
        out_shape=jax.ShapeDtypeStruct((bx, nb, na, c), BF16),
        compiler_params=_cparams("parallel", "arbitrary"),
        name=f"dft1_l{l}",
    )(src.reshape(bx, half, nb, c), mats)


def _s2f_kernel(yf_ref, yb_ref, e2_ref, h_ref, *, nb):
    c = HYENA_WIDTH
    ft = jnp.swapaxes(yf_ref[0].reshape(nb, 2 * SLAB, -1), 0, 1)
    bt = jnp.swapaxes(yb_ref[0].reshape(nb, 2 * SLAB, -1), 0, 1)
    chunk = SLAB if nb <= 64 else SLAB // 2
    for t0 in range(0, SLAB, chunk):
        ts = range(t0, t0 + chunk)
        rows = [jnp.concatenate([src[part * SLAB + t] for t in ts], axis=1)
                for src in (ft, bt) for part in range(2)]
        y = jnp.dot(e2_ref[...], jnp.concatenate(rows, axis=0), preferred_element_type=F32).astype(BF16)
        for k, t in enumerate(ts):
            h_ref[0, t] = y[:, k * c:(k + 1) * c]


def _filter_spectrum(y1, e2, *, na, nb):
    c = HYENA_WIDTH
    y5 = y1.reshape(2 * HYENA_ORDER, nb, 2, na // 2, c)
    spec = lambda d: pl.BlockSpec((1, nb, 2, SLAB, c), lambda o, k: (2 * o + d, 0, 0, k, 0))
    return pl.pallas_call(
        functools.partial(_s2f_kernel, nb=nb),
        grid=(HYENA_ORDER, na // 2 // SLAB),
        in_specs=[spec(0), spec(1), pl.BlockSpec((2 * nb, 4 * nb), lambda o, k: (0, 0))],
        out_specs=pl.BlockSpec((1, SLAB, 2 * nb, c), lambda o, k: (o, k, 0, 0)),
        out_shape=jax.ShapeDtypeStruct((HYENA_ORDER, na // 2, 2 * nb, c), BF16),
        compiler_params=_cparams("parallel", "arbitrary"),
        name=f"filter_spec_n{na * nb}",
    )(y5, y5, e2)


def _s2_kernel(y_ref, h_ref, e2_ref, e2i_ref, out_ref, ot_ref, *, nb):
    c = HYENA_WIDTH // 2
    for c0 in (0, c):
        lanes = slice(c0, c0 + c)
        yt = jnp.swapaxes(y_ref[0, :, :, :, lanes].reshape(nb, 2 * SLAB, c), 0, 1)
        d = jnp.concatenate([jnp.concatenate([yt[part * SLAB + t] for t in range(SLAB)], axis=1)
                             for part in range(2)], axis=0)
        y = jnp.dot(e2_ref[...], d, preferred_element_type=F32).astype(BF16)
        h = jnp.concatenate([h_ref[0, t, :, lanes] for t in range(SLAB)], axis=1)
        yr, yi, hr, hi = y[:nb], y[nb:], h[:nb], h[nb:]
        p = jnp.concatenate([yr * hr - yi * hi, yr * hi + yi * hr], axis=0)
        o = jnp.dot(e2i_ref[...], p, preferred_element_type=F32).astype(BF16)
        for t in range(SLAB):
            ot_ref[t, :, lanes] = o[:nb, t * c:(t + 1) * c]
            ot_ref[SLAB + t, :, lanes] = o[nb:, t * c:(t + 1) * c]
        out_ref[0, :, :, :, lanes] = jnp.swapaxes(ot_ref[:, :, lanes], 0, 1).reshape(nb, 2, SLAB, c)


def _stage2(y1, hspec, order, e2, e2i, *, na, nb):
    bx = y1.shape[0]
    c = HYENA_WIDTH
    y5 = y1.reshape(bx, nb, 2, na // 2, c)
    yspec = pl.BlockSpec((1, nb, 2, SLAB, c), lambda k, bi: (bi, 0, 0, k, 0))
    mat = pl.BlockSpec((2 * nb, 2 * nb), lambda k, bi: (0, 0))
    out = pl.pallas_call(
        functools.partial(_s2_kernel, nb=nb),
        grid=(na // 2 // SLAB, bx),
        in_specs=[yspec, pl.BlockSpec((1, SLAB, 2 * nb, c), lambda k, bi: (order, k, 0, 0)), mat, mat],
        out_specs=yspec,
        out_shape=jax.ShapeDtypeStruct((bx, nb, 2, na // 2, c), BF16),
        scratch_shapes=[pltpu.VMEM((2 * SLAB, nb, c), BF16)],
        compiler_params=_cparams("parallel", "arbitrary"),
        name=f"dft2_n{na * nb}",
    )(y5, hspec, e2, e2i)
    return out.reshape(bx, nb, na, c)


def _s1inv_chain_kernel(y_ref, g_ref, z_ref, gate_ref, d_ref, m_ref, znew_ref, y1_ref, conv_ref):
    zt = jnp.swapaxes(z_ref[0], 0, 1)
    gt = jnp.swapaxes(gate_ref[0], 0, 1)
    rows = zt.shape[0]
    for j in range(rows):
        conv_ref[j] = jnp.dot(g_ref[j], y_ref[0, j], preferred_element_type=F32)
    for j in range(rows):
        znew_ref[0, j] = (gt[j].astype(F32) * (conv_ref[j] + d_ref[...] * zt[j].astype(F32))).astype(BF16)
    for j in range(rows):
        y1_ref[0, j] = jnp.dot(m_ref[j], znew_ref[0, j], preferred_element_type=F32).astype(BF16)


def _s1inv_last_kernel(y_ref, g_ref, z_ref, gate_ref, d_ref, znew_ref, zt_ref):
    gt = jnp.swapaxes(gate_ref[0], 0, 1)
    for j in range(gt.shape[0]):
        conv = jnp.dot(g_ref[j], y_ref[0, j], preferred_element_type=F32)
        zt_ref[j] = (gt[j].astype(F32) * (conv + d_ref[...] * z_ref[0, j].astype(F32))).astype(BF16)
    znew_ref[0] = jnp.swapaxes(zt_ref[...], 0, 1)


def _stage1_inverse(y3, ginv, z, gate, d, mats, *, na, nb):
    bx = y3.shape[0]
    c = HYENA_WIDTH
    half = na // 2
    l = half * nb
    rj = _rows_per_step(half, nb)
    nat = pl.BlockSpec((1, half, rj, c), lambda j, bi: (bi, 0, j, 0))
    tr = lambda rows: pl.BlockSpec((1, rj, rows, c), lambda j, bi: (bi, j, 0, 0))
    gspec = pl.BlockSpec((rj, half, na), lambda j, bi: (j, 0, 0))
    dspec = pl.BlockSpec((1, c), lambda j, bi: (0, 0))
    gate4 = gate.reshape(bx, half, nb, c)
    if mats is not None:
        return pl.pallas_call(
            _s1inv_chain_kernel,
            grid=(nb // rj, bx),
            in_specs=[tr(na), gspec, nat, nat, dspec, pl.BlockSpec((rj, na, half), lambda j, bi: (j, 0, 0))],
            out_specs=[tr(half), tr(na)],
            out_shape=[jax.ShapeDtypeStruct((bx, nb, half, c), BF16),
                       jax.ShapeDtypeStruct((bx, nb, na, c), BF16)],
            scratch_shapes=[pltpu.VMEM((rj, half, c), F32)],
            compiler_params=_cparams("parallel", "arbitrary"),
            name=f"idft1_chain_l{l}",
        )(y3, ginv, z.reshape(bx, half, nb, c), gate4, d, mats)
    out = pl.pallas_call(
        _s1inv_last_kernel,
        grid=(nb // rj, bx),
        in_specs=[tr(na), gspec, tr(half), nat, dspec],
        out_specs=nat,
        out_shape=jax.ShapeDtypeStruct((bx, half, nb, c), BF16),
        scratch_shapes=[pltpu.VMEM((rj, half, c), BF16)],
        compiler_params=_cparams("parallel", "arbitrary"),
        name=f"idft1_last_l{l}",
    )(y3, ginv, z, gate4, d)
    return out.reshape(bx, l, c)


def _hyena(z0, x1, x2, filter_params, hyena_d):
    bx, l, c = z0.shape
    na, nb = _fft_plan(l)
    m_fwd, ginv, e2, e2i, e2fb = (jnp.asarray(t).astype(BF16) for t in _dft_tables(l))
    hspec = _filter_spectrum(_filter_stage1(l, *filter_params, m_fwd, na=na, nb=nb), e2fb, na=na, nb=nb)
    y1 = _stage1(z0, m_fwd, na=na, nb=nb)
    y3 = _stage2(y1, hspec, 0, e2, e2i, na=na, nb=nb)
    z1, y1 = _stage1_inverse(y3, ginv, z0, x1, hyena_d[0:1], m_fwd, na=na, nb=nb)
    y3 = _stage2(y1, hspec, 1, e2, e2i, na=na, nb=nb)
    return _stage1_inverse(y3, ginv, z1, x2, hyena_d[1:2], None, na=na, nb=nb)


def _final_kernel(attn_ref, ga_ref, z_ref, gh_ref, x_ref, ag_ref, hg_ref, wo_ref, lg_ref, lb_ref, y_ref):
    def rms(v, g):
        return v * lax.rsqrt(jnp.mean(v * v, axis=-1, keepdims=True) + RMS_EPS) * g

    ma = (rms(attn_ref[0].astype(F32), ag_ref[...]) * ga_ref[0].astype(F32)).astype(BF16)
    mh = (rms(z_ref[0].astype(F32), hg_ref[...]) * gh_ref[0].astype(F32)).astype(BF16)
    a = ATTN_WIDTH
    out = (jnp.dot(ma, wo_ref[0:a, :], preferred_element_type=F32)
           + jnp.dot(mh, wo_ref[a:, :], preferred_element_type=F32))
    h = ((2.0 * DEPTH) ** 0.25) * x_ref[0] + out
    mu = jnp.mean(h, axis=-1, keepdims=True)
    hc = h - mu
    var = jnp.mean(hc * hc, axis=-1, keepdims=True)
    y_ref[0] = hc * lax.rsqrt(var + LN_EPS) * lg_ref[...] + lb_ref[...]


def _final(attn, qkvg, z, gh, x, attn_g, hyena_g, w_out_bf, ln_g, ln_b, *, tt):
    b, l, d = x.shape
    a = ATTN_WIDTH
    tok = lambda w, col=0: pl.BlockSpec((1, tt, w), lambda bi, i: (bi, i, col))
    vec = lambda w: pl.BlockSpec((1, w), lambda bi, i: (0, 0))
    return pl.pallas_call(
        _final_kernel,
        grid=(b, l // tt),
        in_specs=[tok(a), tok(a, 3), tok(a), tok(a), tok(d),
                  vec(a), vec(a), pl.BlockSpec((d, d), lambda bi, i: (0, 0)), vec(d), vec(d)],
        out_specs=tok(d),
        out_shape=jax.ShapeDtypeStruct((b, l, d), F32),
        compiler_params=_cparams("parallel", "arbitrary"),
        name="outproj_ln",
    )(attn, qkvg, z, gh, x, attn_g[None, :], hyena_g[None, :], w_out_bf, ln_g[None, :], ln_b[None, :])


def _layer(x, w_in_bf, conv_w, conv_b, filt_w1, filt_b1, filt_w2, filt_b2, filt_w3, filt_b3,
           filt_freq, filt_w4, hyena_d, attn_norm_g, hyena_norm_g, w_out_bf, ln_g, ln_b):
    b, l, _ = x.shape
    tm = min(1024, l)
    qkvg, z0, x1, x2, gh = _inproj(x, w_in_bf, conv_w, conv_b[None, :], tm=tm)
    attn = _attention(qkvg)
    z = _hyena(z0, x1, x2, (filt_w1, filt_b1, filt_w2, filt_b2, filt_w3, filt_b3, filt_freq, filt_w4), hyena_d)
    return _final(attn, qkvg, z, gh, x, attn_norm_g, hyena_norm_g, w_out_bf, ln_g, ln_b, tt=tm)


def kernel(x_prompt, x_sample, w_in, conv_w, conv_b, filt_w1, filt_b1, filt_w2, filt_b2, filt_w3,
           filt_b3, filt_freq, filt_w4, hyena_d, attn_norm_g, hyena_norm_g, w_out, ln_g, ln_b):
    def trunk(x):
        for i in range(DEPTH):
            x = _layer(x, w_in[i].astype(BF16), conv_w[i], conv_b[i], filt_w1[i], filt_b1[i], filt_w2[i],
                       filt_b2[i], filt_w3[i], filt_b3[i], filt_freq[i], filt_w4[i], hyena_d[i],
                       attn_norm_g[i], hyena_norm_g[i], w_out[i].astype(BF16), ln_g[i], ln_b[i])
        return x
    return (trunk(x_prompt), trunk(x_sample))
```
